```python
import jax, jax.numpy as jnp
from jax import lax
import numpy as np

D_MODEL = 1024
BATCH = 32
SEQ = 2048
DEPTH = 4

CHUNK = 64
D_FF = 2816
SSD_WIDTH = D_MODEL // 2
SSD_HEAD_DIM = 64
SSD_HEADS = SSD_WIDTH // SSD_HEAD_DIM
SSD_GROUPS = 2
SSD_STATE = 64
SSD_CONV = 4
SSD_CONV_DIM = SSD_WIDTH + 2 * SSD_GROUPS * SSD_STATE
HG_WIDTH = D_MODEL - SSD_WIDTH
HG_HEADS = 8
HG_VAL_HEAD = HG_WIDTH // HG_HEADS
HG_KEY_HEAD = 64
HG_KEY_DIM = HG_HEADS * HG_KEY_HEAD
HG_CHUNK = 16
D_MIX = SSD_WIDTH + HG_WIDTH
D_PROJ = SSD_WIDTH + SSD_CONV_DIM + SSD_HEADS + 2 * HG_KEY_DIM + 2 * HG_WIDTH
NORM_EPS = 1e-6

kernel_name = "hymba_ssd_hgrn2_macaron_trunk"


def rmsnorm(x, w):
    xf = x.astype(jnp.float32)
    xf = xf * lax.rsqrt(jnp.mean(xf * xf, axis=-1, keepdims=True) + NORM_EPS)
    return (xf * w.astype(jnp.float32)).astype(x.dtype)


def swiglu(h, w_gate, w_up, w_down):
    return (jax.nn.silu(h @ w_gate) * (h @ w_up)) @ w_down


def chunk_state_scan(decay, contrib):
    decay_t = jnp.moveaxis(decay, 1, 0)
    contrib_t = jnp.moveaxis(contrib, 1, 0)

    def step(state, inp):
        d, c = inp
        return d * state + c, state

    _, prev = lax.scan(step, jnp.zeros_like(contrib_t[0]), (decay_t, contrib_t))
    return jnp.moveaxis(prev, 0, 1)


def causal_depthwise_conv(u, w, b):
    L = u.shape[1]
    upad = jnp.pad(u, ((0, 0), (SSD_CONV - 1, 0), (0, 0)))
    out = sum(upad[:, k:k + L] * w[k] for k in range(SSD_CONV))
    return out + b


def ssd_mixer(z, xbc_raw, dt_raw, conv_w, conv_b, dt_bias, a_log, d_skip, norm_w):
    Bsz, L, _ = z.shape
    nc = L // CHUNK
    R = SSD_HEADS // SSD_GROUPS
    xbc = jax.nn.silu(causal_depthwise_conv(xbc_raw, conv_w, conv_b))
    xs, Bm, Cm = jnp.split(xbc, [SSD_WIDTH, SSD_WIDTH + SSD_GROUPS * SSD_STATE], axis=-1)
    dt = jax.nn.softplus(dt_raw + dt_bias)
    A = -jnp.exp(a_log).reshape(SSD_GROUPS, R)
    xs = xs.reshape(Bsz, nc, CHUNK, SSD_GROUPS, R, SSD_HEAD_DIM)
    Bm = Bm.reshape(Bsz, nc, CHUNK, SSD_GROUPS, SSD_STATE)
    Cm = Cm.reshape(Bsz, nc, CHUNK, SSD_GROUPS, SSD_STATE)
    dt = dt.reshape(Bsz, nc, CHUNK, SSD_GROUPS, R)
    x_dt = xs * dt[..., None]
    a_cum = jnp.cumsum(dt * A, axis=2)
    causal = jnp.tril(jnp.ones((CHUNK, CHUNK), dtype=bool))[None, None, :, :, None, None]
    seg = a_cum[:, :, :, None] - a_cum[:, :, None, :]
    decay_ts = jnp.where(causal, jnp.exp(jnp.where(causal, seg, 0.0)), 0.0)
    cb = jnp.einsum('bctgn,bcsgn->bctsg', Cm, Bm)
    y_intra = jnp.einsum('bctsg,bctsgr,bcsgrp->bctgrp', cb, decay_ts, x_dt)
    decay_end = jnp.exp(a_cum[:, :, -1:] - a_cum)
    contrib = jnp.einsum('bcsgn,bcsgr,bcsgrp->bcgrpn', Bm, decay_end, x_dt)
    chunk_decay = jnp.exp(a_cum[:, :, -1])[..., None, None]
    s_prev = chunk_state_scan(chunk_decay, contrib)
    y_inter = jnp.einsum('bctgn,bcgrpn,bctgr->bctgrp', Cm, s_prev, jnp.exp(a_cum))
    y = y_intra + y_inter + d_skip.reshape(SSD_GROUPS, R)[:, :, None] * xs
    y = y.reshape(Bsz, L, SSD_WIDTH) * jax.nn.silu(z)
    yg = y.reshape(Bsz, L, SSD_GROUPS, SSD_WIDTH // SSD_GROUPS)
    yg = yg * lax.rsqrt(jnp.mean(yg * yg, axis=-1, keepdims=True) + NORM_EPS)
    return yg.reshape(Bsz, L, SSD_WIDTH) * norm_w


def hgrn2_mixer(q_raw, f_raw, i_raw, g_raw, lb, norm_w):
    Bsz, L, _ = q_raw.shape
    nc = L // HG_CHUNK
    kshp = (Bsz, nc, HG_CHUNK, HG_HEADS, HG_KEY_HEAD)
    lbh = lb.reshape(HG_HEADS, HG_KEY_HEAD)
    a = f_raw.reshape(kshp)
    q = jax.nn.silu(q_raw).reshape(kshp)
    f = lbh + (1.0 - lbh) * jax.nn.sigmoid(a)
    logf = jnp.log(f)
    k = (1.0 - lbh) * jax.nn.sigmoid(-a)
    v = i_raw.reshape(Bsz, nc, HG_CHUNK, HG_HEADS, HG_VAL_HEAD)
    b = jnp.cumsum(logf, axis=2)
    b_last = b[:, :, -1:]
    b_ref = b[:, :, HG_CHUNK // 2 - 1:HG_CHUNK // 2]
    qe_rel = q * jnp.exp(b - b_ref)
    ke_rel = k * jnp.exp(b_ref - b)
    kd = k * jnp.exp(b_last - b)
    mask = jnp.tril(jnp.ones((HG_CHUNK, HG_CHUNK), dtype=bool))
    scores = jnp.einsum('bcthd,bcshd->bchts', qe_rel, ke_rel)
    scores = jnp.where(mask, scores, 0.0)
    o_intra = jnp.einsum('bchts,bcshv->bcthv', scores, v)
    contrib = jnp.einsum('bcshd,bcshv->bchdv', kd, v)
    decay = jnp.exp(b_last[:, :, 0])[..., None]
    s_prev = chunk_state_scan(decay, contrib)
    o_inter = jnp.einsum('bcthd,bchdv->bcthv', q * jnp.exp(b), s_prev)
    o = (o_intra + o_inter).reshape(Bsz, L, HG_HEADS, HG_VAL_HEAD)
    o = o * lax.rsqrt(jnp.mean(o * o, axis=-1, keepdims=True) + NORM_EPS) * norm_w
    o = o * jax.nn.silu(g_raw.reshape(Bsz, L, HG_HEADS, HG_VAL_HEAD))
    return o.reshape(Bsz, L, HG_WIDTH)


def setup_inputs(seed: int = 0) -> dict:
    key = jax.random.key(seed)
    ks = jax.random.split(key, 24)
    f32 = jnp.float32

    def nrm(k, shape, fan_in):
        return jax.random.normal(k, shape, f32) * (fan_in ** -0.5)

    def gain(k, shape):
        return 1.0 + 0.05 * jax.random.normal(k, shape, f32)

    u = jax.random.uniform(ks[11], (DEPTH, SSD_HEADS), f32)
    dt0 = jnp.exp(u * (jnp.log(0.1) - jnp.log(0.001)) + jnp.log(0.001))
    dt_bias = dt0 + jnp.log(-jnp.expm1(-dt0))
    return {
        "x": jax.random.normal(ks[0], (BATCH, SEQ, D_MODEL), f32),
        "ffn1_norm": gain(ks[1], (DEPTH, D_MODEL)),
        "ffn1_w_gate": nrm(ks[2], (DEPTH, D_MODEL, D_FF), D_MODEL),
        "ffn1_w_up": nrm(ks[3], (DEPTH, D_MODEL, D_FF), D_MODEL),
        "ffn1_w_down": nrm(ks[4], (DEPTH, D_FF, D_MODEL), D_FF),
        "mix_norm": gain(ks[5], (DEPTH, D_MODEL)),
        "w_in": nrm(ks[6], (DEPTH, D_MODEL, D_PROJ), D_MODEL),
        "ssd_conv_w": nrm(ks[7], (DEPTH, SSD_CONV, SSD_CONV_DIM), SSD_CONV),
        "ssd_conv_b": 0.02 * jax.random.normal(ks[8], (DEPTH, SSD_CONV_DIM), f32),
        "ssd_dt_bias": dt_bias,
        "ssd_a_log": jnp.log(jax.random.uniform(ks[9], (DEPTH, SSD_HEADS), f32, 1.0, 16.0)),
        "ssd_d": gain(ks[10], (DEPTH, SSD_HEADS)),
        "ssd_norm_w": gain(ks[12], (DEPTH, SSD_WIDTH)),
        "hg_lb_logits": 0.5 * jax.random.normal(ks[13], (DEPTH, HG_KEY_DIM), f32),
        "hg_norm_w": gain(ks[14], (DEPTH, HG_VAL_HEAD)),
        "w_out": nrm(ks[15], (DEPTH, D_MIX, D_MODEL), D_MIX),
        "ffn2_norm": gain(ks[16], (DEPTH, D_MODEL)),
        "ffn2_w_gate": nrm(ks[17], (DEPTH, D_MODEL, D_FF), D_MODEL),
        "ffn2_w_up": nrm(ks[18], (DEPTH, D_MODEL, D_FF), D_MODEL),
        "ffn2_w_down": nrm(ks[19], (DEPTH, D_FF, D_MODEL), D_FF),
        "final_norm": gain(ks[20], (D_MODEL,)),
    }


def reference(x, ffn1_norm, ffn1_w_gate, ffn1_w_up, ffn1_w_down, mix_norm, w_in,
              ssd_conv_w, ssd_conv_b, ssd_dt_bias, ssd_a_log, ssd_d, ssd_norm_w,
              hg_lb_logits, hg_norm_w, w_out, ffn2_norm, ffn2_w_gate, ffn2_w_up,
              ffn2_w_down, final_norm):
    f32 = jnp.float32
    lb_p = jax.nn.softmax(hg_lb_logits.astype(f32), axis=0)
    lower_bounds = jnp.cumsum(lb_p, axis=0) - lb_p[0]
    split_idx = list(np.cumsum([SSD_WIDTH, SSD_CONV_DIM, SSD_HEADS,
                                HG_KEY_DIM, HG_KEY_DIM, HG_WIDTH])[:])
    for l in range(DEPTH):
        x = x + 0.5 * swiglu(rmsnorm(x, ffn1_norm[l]), ffn1_w_gate[l], ffn1_w_up[l], ffn1_w_down[l])
        h = rmsnorm(x, mix_norm[l])
        proj = (h @ w_in[l]).astype(f32)
        z, xbc, dt_raw, q_raw, f_raw, i_raw, g_raw = jnp.split(proj, split_idx, axis=-1)
        y_ssd = ssd_mixer(z, xbc, dt_raw, ssd_conv_w[l].astype(f32), ssd_conv_b[l].astype(f32),
                          ssd_dt_bias[l].astype(f32), ssd_a_log[l].astype(f32),
                          ssd_d[l].astype(f32), ssd_norm_w[l].astype(f32))
        y_hg = hgrn2_mixer(q_raw, f_raw, i_raw, g_raw, lower_bounds[l], hg_norm_w[l].astype(f32))
        y = jnp.concatenate([y_ssd, y_hg], axis=-1).astype(x.dtype)
        x = x + y @ w_out[l]
        x = x + 0.5 * swiglu(rmsnorm(x, ffn2_norm[l]), ffn2_w_gate[l], ffn2_w_up[l], ffn2_w_down[l])
    return rmsnorm(x, final_norm)
```

```python
import functools

import numpy as np
import jax
import jax.numpy as jnp
from jax import lax
from jax.experimental import pallas as pl
from jax.experimental.pallas import tpu as pltpu

F32 = jnp.float32
BF16 = jnp.bfloat16

D_MODEL = 1024
D_FF = 2816
SSD_W = 512
SSD_GN = 128
CONV_DIM = SSD_W + 2 * SSD_GN
CONV_K = 4
HG_W = 512
HEAD = 64
WIN = 64
HG_CHUNK = 16
PACK = 256
NORM_EPS = 1e-6

D_PROJ_IN = 3336
D_PROJ = 3840

FFN_ROWS = 512
FFN_COLS = 256
MIX_ROWS = 256
VMEM_LIMIT = 56 * 1024 * 1024

NT_DIMS = (((1,), (1,)), ((), ()))
TN_DIMS = (((0,), (0,)), ((), ()))


def _rms(x, w):
    return x * lax.rsqrt(jnp.mean(x * x, axis=-1, keepdims=True) + NORM_EPS) * w


def _silu(x):
    return x * jax.nn.sigmoid(x)


def _dot(a, b):
    return jnp.dot(a, b, preferred_element_type=F32)


def _split3(x):
    hi = x.astype(BF16)
    r1 = x - hi.astype(F32)
    mid = r1.astype(BF16)
    lo = (r1 - mid.astype(F32)).astype(BF16)
    return hi, mid, lo


def _dot_exact_lhs(m, x):
    hi, mid, lo = _split3(x)
    return _dot(m, lo) + _dot(m, mid) + _dot(m, hi)


def _ffn_body(x_ref, nw_ref, wg_ref, wu_ref, wd_ref, fnw_ref, o_ref, h_ref, *, final):
    x = x_ref[...]
    xn = _rms(x, nw_ref[...]).astype(BF16)
    for c in range(D_FF // FFN_COLS):
        cols = slice(c * FFN_COLS, (c + 1) * FFN_COLS)
        g = _dot(xn, wg_ref[:, cols])
        u = _dot(xn, wu_ref[:, cols])
        h_ref[:, cols] = (_silu(g) * u).astype(BF16)
    y = x + 0.5 * _dot(h_ref[...], wd_ref[...])
    if final:
        y = _rms(y, fnw_ref[...])
    o_ref[...] = y


def _ffn(x2d, nw, wg, wu, wd, fnw, layer, final):
    n = x2d.shape[0]
    const = lambda i: (0, 0)
    wsel = lambda i: (layer, 0, 0)
    return pl.pallas_call(
        functools.partial(_ffn_body, final=final),
        grid=(n // FFN_ROWS,),
        in_specs=[
            pl.BlockSpec((FFN_ROWS, D_MODEL), lambda i: (i, 0)),
            pl.BlockSpec((None, 1, D_MODEL), wsel),
            pl.BlockSpec((None, D_MODEL, D_FF), wsel),
            pl.BlockSpec((None, D_MODEL, D_FF), wsel),
            pl.BlockSpec((None, D_FF, D_MODEL), wsel),
            pl.BlockSpec((1, D_MODEL), const),
        ],
        out_specs=pl.BlockSpec((FFN_ROWS, D_MODEL), lambda i: (i, 0)),
        out_shape=jax.ShapeDtypeStruct((n, D_MODEL), F32),
        scratch_shapes=[pltpu.VMEM((FFN_ROWS, D_FF), BF16)],
        compiler_params=pltpu.CompilerParams(
            dimension_semantics=("arbitrary",), vmem_limit_bytes=VMEM_LIMIT),
        name="ffn_half_step",
    )(x2d, nw, wg, wu, wd, fnw)


def _mixer_body(x_ref, nw_ref, win_ref, cw_ref, cb_ref, dtb_ref, alog_ref, dsk_ref, snw_ref,
                lbl_ref, hnw_ref, wout_ref, l64_ref, l16_ref, slm_ref, causal_ref, code_ref,
                bd4_ref, gmask_ref,
                o_ref,
                proj_ref, xbc_ref, act_ref, dt_ref, acum_ref, seg_ref, q_ref, k_ref, b_ref,
                y_ref, sst_ref, st_ref, *, layer):
    rows_t = MIX_ROWS

    @pl.when(pl.program_id(1) == 0)
    def _():
        xbc_ref[0:8, :] = jnp.zeros((8, CONV_DIM), F32)
        sst_ref[...] = jnp.zeros_like(sst_ref)
        st_ref[...] = jnp.zeros_like(st_ref)

    x = x_ref[...]
    hn = _rms(x, nw_ref[...]).astype(BF16)
    proj_ref[:, 0:SSD_W] = _dot(hn, win_ref[:, 0:SSD_W])
    xbc_ref[8:8 + rows_t, :] = _dot(hn, win_ref[:, SSD_W:SSD_W + CONV_DIM])
    for c in range(5):
        src = SSD_W + CONV_DIM + c * 512
        proj_ref[:, SSD_W + c * 512:SSD_W + (c + 1) * 512] = _dot(hn, win_ref[:, src:src + 512])
    c_q, c_f, c_i, c_g, c_dt = 512, 1024, 1536, 2048, 2560

    acc = cb_ref[...] + cw_ref[0:1, :] * xbc_ref[5:5 + rows_t, :]
    for kk in range(1, CONV_K):
        acc = acc + cw_ref[kk:kk + 1, :] * xbc_ref[5 + kk:5 + kk + rows_t, :]
    act_ref[...] = _silu(acc)
    xbc_ref[0:8, :] = xbc_ref[rows_t:rows_t + 8, :]

    draw = proj_ref[:, c_dt:c_dt + SSD_W] + dtb_ref[...]
    dtv = jnp.maximum(draw, 0.0) + jnp.log1p(jnp.exp(-jnp.abs(draw)))
    dt_ref[...] = dtv
    d_a = dtv * (-jnp.exp(alog_ref[...]))
    acum_ref[...] = _dot_exact_lhs(l64_ref[...], d_a)
    seg_ref[...] = _dot_exact_lhs(l64_ref[...], d_a * slm_ref[...])

    p = jax.nn.softmax(lbl_ref[...], axis=0)
    lb = jnp.zeros((1, HG_W), F32)
    for j in range(1, layer + 1):
        lb = lb + p[j:j + 1, :]
    fraw = proj_ref[:, c_f:c_f + HG_W]
    logf = jnp.log(lb + (1.0 - lb) * jax.nn.sigmoid(fraw))
    k_ref[...] = (1.0 - lb) * jax.nn.sigmoid(-fraw)
    qraw = proj_ref[:, c_q:c_q + HG_W]
    q_ref[...] = _silu(qraw)
    b_ref[...] = _dot_exact_lhs(l16_ref[...], logf)

    bd4 = bd4_ref[...] != 0
    lane_gn = lax.broadcasted_iota(jnp.int32, (WIN, SSD_GN), 1)

    def window(w, carry):
        r0 = pl.multiple_of(w * WIN, WIN)
        rows = pl.ds(r0, WIN)

        xs = act_ref[rows, 0:SSD_W]
        bm = act_ref[rows, SSD_W:SSD_W + SSD_GN]
        cm = act_ref[rows, SSD_W + SSD_GN:CONV_DIM].astype(BF16)
        dtw = dt_ref[rows, :]
        ac = acum_ref[rows, :]
        a_last = ac[WIN - 1:WIN, :]
        xdt = xs * dtw
        xdd = (xdt * jnp.exp(a_last - ac)).astype(BF16)
        bm0 = jnp.where(lane_gn < HEAD, bm, 0.0).astype(BF16)
        bm1 = jnp.where(lane_gn >= HEAD, bm, 0.0).astype(BF16)
        bm_bd = jnp.concatenate([bm0] * 4 + [bm1] * 4, axis=0)
        cbw = lax.dot_general(cm, bm_bd, NT_DIMS, preferred_element_type=F32)
        mw = jnp.where(causal_ref[...] != 0, cbw * jnp.exp(seg_ref[rows, :]), 0.0).astype(BF16)
        y_parts = []
        for hg in range(2):
            cs = slice(hg * PACK, (hg + 1) * PACK)
            xg = xdt[:, cs].astype(BF16)
            x_bd = jnp.where(bd4, jnp.concatenate([xg] * 4, axis=0), 0.0)
            y_parts.append(_dot(mw[:, cs], x_bd))
        y = jnp.concatenate(y_parts, axis=1)
        s_prev = sst_ref[...]
        s_use = jnp.where(gmask_ref[...] != 0, s_prev, 0.0).astype(BF16)
        y = y + _dot(cm, s_use) * jnp.exp(ac)
        contrib = lax.dot_general(bm.astype(BF16), xdd, TN_DIMS, preferred_element_type=F32)
        sst_ref[...] = s_prev * jnp.exp(a_last) + contrib
        y = y + dsk_ref[...] * xs
        y = y * _silu(proj_ref[rows, 0:SSD_W])
        y2 = y * y
        ms0 = jnp.mean(y2[:, 0:PACK], axis=-1, keepdims=True)
        ms1 = jnp.mean(y2[:, PACK:2 * PACK], axis=-1, keepdims=True)
        inv = jnp.concatenate([jnp.broadcast_to(lax.rsqrt(ms0 + NORM_EPS), (WIN, PACK)),
                               jnp.broadcast_to(lax.rsqrt(ms1 + NORM_EPS), (WIN, PACK))], axis=1)
        y_ref[rows, 0:SSD_W] = (y * inv * snw_ref[...]).astype(BF16)

        code = code_ref[...]
        for hg in range(2):
            cs = slice(hg * PACK, (hg + 1) * PACK)
            q = q_ref[rows, cs]
            k = k_ref[rows, cs]
            b = b_ref[rows, cs]
            v = proj_ref[rows, c_i + hg * PACK:c_i + (hg + 1) * PACK].astype(BF16)
            nch = WIN // HG_CHUNK
            bl = [b[HG_CHUNK * c + HG_CHUNK - 1:HG_CHUNK * c + HG_CHUNK, :] for c in range(nch)]
            br = [b[HG_CHUNK * c + HG_CHUNK // 2 - 1:HG_CHUNK * c + HG_CHUNK // 2, :] for c in range(nch)]
            zero = jnp.zeros_like(bl[0])
            gpre = [zero, bl[0], bl[0] + bl[1], bl[0] + bl[1] + bl[2]]
            gtot = gpre[3] + bl[3]
            e2 = [zero, bl[0], bl[1], bl[2]]
            e3 = [zero, zero, bl[1] + bl[0], bl[2] + bl[1]]

            def bc(rows_list):
                return jnp.concatenate([jnp.broadcast_to(r, (HG_CHUNK, PACK)) for r in rows_list], axis=0)

            bl_bc, br_bc, g_bc = bc(bl), bc(br), bc(gpre)
            qe = (q * jnp.exp(b - br_bc)).astype(BF16)
            ke = (k * jnp.exp(br_bc - b)).astype(BF16)
            kd = k * jnp.exp(bl_bc - b)
            qb = q * jnp.exp(b)
            q2 = qb * jnp.exp(bc(e2))
            q3 = qb * jnp.exp(bc(e3))
            qg = (qb * jnp.exp(g_bc)).astype(BF16)
            k64 = (kd * jnp.exp(gtot - g_bc - bl_bc)).astype(BF16)
            ke_bd = jnp.where(bd4, jnp.concatenate([ke] * 4, axis=0), 0.0)
            kd_bd = jnp.where(bd4, jnp.concatenate([kd.astype(BF16)] * 4, axis=0), 0.0)
            v_bd = jnp.where(bd4, jnp.concatenate([v] * 4, axis=0), 0.0)
            s0 = lax.dot_general(qe, ke_bd, NT_DIMS, preferred_element_type=F32)
            qs = jnp.concatenate([qb, q2, q3], axis=0).astype(BF16)
            s123 = lax.dot_general(qs, kd_bd, NT_DIMS, preferred_element_type=F32)
            sc = jnp.where(code == 0, s0,
                           jnp.where(code == 1, s123[0:WIN],
                                     jnp.where(code == 2, s123[WIN:2 * WIN],
                                               jnp.where(code == 3, s123[2 * WIN:3 * WIN], 0.0))))
            o = _dot(sc.astype(BF16), v_bd)
            st = st_ref[hg]
            st_bd = jnp.where(bd4, st, 0.0).astype(BF16)
            o = o + lax.dot_general(qg, st_bd, NT_DIMS, preferred_element_type=F32)
            contrib = lax.dot_general(v, k64, TN_DIMS, preferred_element_type=F32)
            st_ref[hg] = st * jnp.exp(gtot) + contrib
            oo = o * o
            oh = oo.astype(BF16)
            ol = (oo - oh.astype(F32)).astype(BF16)
            ms = (_dot(ol, bd4_ref[...]) + _dot(oh, bd4_ref[...])) * (1.0 / HEAD)
            gate = _silu(proj_ref[rows, c_g + hg * PACK:c_g + (hg + 1) * PACK])
            yh = o * lax.rsqrt(ms + NORM_EPS) * hnw_ref[:, cs] * gate
            y_ref[rows, SSD_W + hg * PACK:SSD_W + (hg + 1) * PACK] = yh.astype(BF16)
        return carry

    lax.fori_loop(0, rows_t // WIN, window, 0)

    o_ref[...] = x + _dot(y_ref[...], wout_ref[...])


def _mixer_consts():
    t = MIX_ROWS
    r = np.arange(t)
    same64 = (r[:, None] // WIN) == (r[None, :] // WIN)
    same16 = (r[:, None] // HG_CHUNK) == (r[None, :] // HG_CHUNK)
    lower = r[None, :] <= r[:, None]
    l64 = (same64 & lower).astype(np.float32)
    l16 = (same16 & lower).astype(np.float32)
    lane = np.arange(SSD_W)
    slm = ((r[:, None] % WIN) > (lane[None, :] % HEAD)).astype(np.float32)
    tt = np.arange(WIN)
    causal = (tt[:, None] >= (lane[None, :] % HEAD)).astype(np.float32)
    ls = np.arange(PACK) % HEAD
    dch = tt[:, None] // HG_CHUNK - ls[None, :] // HG_CHUNK
    code = np.where(dch == 0, np.where(ls[None, :] <= tt[:, None], 0, 9), np.where((dch >= 1) & (dch <= 3), dch, 9))
    rp = np.arange(PACK)
    bd4 = ((rp[:, None] // HEAD) == (rp[None, :] // HEAD)).astype(np.float32)
    gn = np.arange(SSD_GN)
    gmask = ((gn[:, None] // HEAD) == (lane[None, :] // PACK)).astype(np.float32)
    return (jnp.asarray(l64, BF16), jnp.asarray(l16, BF16), jnp.asarray(slm, F32), jnp.asarray(causal, F32),
            jnp.asarray(code.astype(np.float32), F32), jnp.asarray(bd4, BF16), jnp.asarray(gmask, F32))


def _mixer(x3d, nw, win, cw, cb, dtb, alog, dsk, snw, lbl, hnw, wout, consts, layer):
    bsz, seq, _ = x3d.shape
    t = MIX_ROWS
    const2 = lambda b, j: (0, 0)
    lsel = lambda b, j: (layer, 0, 0)
    full = lambda a: pl.BlockSpec(a.shape, const2)
    l64, l16, slm, causal, code, bd4, gmask = consts
    return pl.pallas_call(
        functools.partial(_mixer_body, layer=layer),
        grid=(bsz, seq // t),
        in_specs=[
            pl.BlockSpec((None, t, D_MODEL), lambda b, j: (b, j, 0)),
            pl.BlockSpec((None, 1, D_MODEL), lsel),
            pl.BlockSpec((None, D_MODEL, D_PROJ), lsel),
            pl.BlockSpec((None, CONV_K, CONV_DIM), lsel),
            pl.BlockSpec((None, 1, CONV_DIM), lsel),
            pl.BlockSpec((None, 1, SSD_W), lsel),
            pl.BlockSpec((None, 1, SSD_W), lsel),
            pl.BlockSpec((None, 1, SSD_W), lsel),
            pl.BlockSpec((None, 1, SSD_W), lsel),
            full(lbl),
            pl.BlockSpec((None, 1, HG_W), lsel),
            pl.BlockSpec((None, D_MODEL, D_MODEL), lsel),
            full(l64), full(l16), full(slm), full(causal), full(code), full(bd4), full(gmask),
        ],
        out_specs=pl.BlockSpec((None, t, D_MODEL), lambda b, j: (b, j, 0)),
        out_shape=jax.ShapeDtypeStruct(x3d.shape, F32),
        scratch_shapes=[
            pltpu.VMEM((t, 3072), F32),
            pltpu.VMEM((t + 8, CONV_DIM), F32),
            pltpu.VMEM((t, CONV_DIM), F32),
            pltpu.VMEM((t, SSD_W), F32),
            pltpu.VMEM((t, SSD_W), F32),
            pltpu.VMEM((t, SSD_W), F32),
            pltpu.VMEM((t, HG_W), F32),
            pltpu.VMEM((t, HG_W), F32),
            pltpu.VMEM((t, HG_W), F32),
            pltpu.VMEM((t, D_MODEL), BF16),
            pltpu.VMEM((SSD_GN, SSD_W), F32),
            pltpu.VMEM((2, PACK, PACK), F32),
        ],
        compiler_params=pltpu.CompilerParams(
            dimension_semantics=("arbitrary", "arbitrary"), vmem_limit_bytes=VMEM_LIMIT),
        name="token_mixer",
    )(x3d, nw, win, cw, cb, dtb, alog, dsk, snw, lbl, hnw, wout, l64, l16, slm, causal, code, bd4, gmask)


def kernel(x, ffn1_norm, ffn1_w_gate, ffn1_w_up, ffn1_w_down, mix_norm, w_in, ssd_conv_w, ssd_conv_b,
           ssd_dt_bias, ssd_a_log, ssd_d, ssd_norm_w, hg_lb_logits, hg_norm_w, w_out, ffn2_norm,
           ffn2_w_gate, ffn2_w_up, ffn2_w_down, final_norm):
    bsz, seq, _ = x.shape
    depth = w_in.shape[0]
    rep = lambda a: jnp.repeat(a.astype(F32), HEAD, axis=-1)[:, None, :]
    row = lambda a: a.astype(F32)[:, None, :]
    w_dt = jnp.repeat(w_in[:, :, 1280:1288], HEAD, axis=-1)
    win_k = jnp.concatenate([w_in[:, :, 0:1280], w_in[:, :, 1288:D_PROJ_IN], w_dt], axis=-1).astype(BF16)
    wout_k = w_out.astype(BF16)
    f1 = (row(ffn1_norm), ffn1_w_gate.astype(BF16), ffn1_w_up.astype(BF16), ffn1_w_down.astype(BF16))
    f2 = (row(ffn2_norm), ffn2_w_gate.astype(BF16), ffn2_w_up.astype(BF16), ffn2_w_down.astype(BF16))
    fnw = final_norm.astype(F32)[None, :]
    hnw = jnp.tile(hg_norm_w.astype(F32), (1, HG_W // HEAD))[:, None, :]
    consts = _mixer_consts()

    xf = x.astype(F32)
    for l in range(depth):
        x2 = _ffn(xf.reshape(bsz * seq, D_MODEL), *f1, fnw, l, False)
        x3 = _mixer(x2.reshape(bsz, seq, D_MODEL), row(mix_norm), win_k, ssd_conv_w.astype(F32),
                    row(ssd_conv_b), rep(ssd_dt_bias), rep(ssd_a_log), rep(ssd_d), row(ssd_norm_w),
                    hg_lb_logits.astype(F32), hnw, wout_k, consts, l)
        xf = _ffn(x3.reshape(bsz * seq, D_MODEL), *f2, fnw, l, l == depth - 1).reshape(bsz, seq, D_MODEL)
    return xf.astype(x.dtype)
```

```python
import functools

import numpy as np
import jax
import jax.numpy as jnp
from jax import lax
from jax.experimental import pallas as pl
from jax.experimental.pallas import tpu as pltpu

F32 = jnp.float32
BF16 = jnp.bfloat16

D_MODEL = 1024
D_FF = 2816
SSD_W = 512
SSD_GN = 128
CONV_DIM = SSD_W + 2 * SSD_GN
CONV_K = 4
HG_W = 512
HEAD = 64
WIN = 64
HG_CHUNK = 16
PACK = 256
NORM_EPS = 1e-6

D_PROJ_IN = 3336
D_PROJ = 3840

FFN_ROWS = 512
FFN_COLS = 256
MIX_ROWS = 256
VMEM_LIMIT = 56 * 1024 * 1024

NT_DIMS = (((1,), (1,)), ((), ()))
TN_DIMS = (((0,), (0,)), ((), ()))


def _rms(x, w):
    return x * lax.rsqrt(jnp.mean(x * x, axis=-1, keepdims=True) + NORM_EPS) * w


def _silu(x):
    return x * jax.nn.sigmoid(x)


def _dot(a, b):
    return jnp.dot(a, b, preferred_element_type=F32)


def _dot_exact_lhs(mm, x):
    hi = x.astype(BF16)
    lo = (x - hi.astype(F32)).astype(BF16)
    return _dot(mm, jnp.concatenate([hi, lo], axis=0))


def _ffn_body(x_ref, nw_ref, wg_ref, wu_ref, wd_ref, fnw_ref, o_ref, h_ref, *, final):
    x = x_ref[...]
    xn = _rms(x, nw_ref[...]).astype(BF16)
    for c in range(D_FF // FFN_COLS):
        cols = slice(c * FFN_COLS, (c + 1) * FFN_COLS)
        g = _dot(xn, wg_ref[:, cols])
        u = _dot(xn, wu_ref[:, cols])
        h_ref[:, cols] = (_silu(g) * u).astype(BF16)
    y = x + 0.5 * _dot(h_ref[...], wd_ref[...])
    if final:
        y = _rms(y, fnw_ref[...])
    o_ref[...] = y


def _ffn(x2d, nw, wg, wu, wd, fnw, layer, final):
    n = x2d.shape[0]
    const = lambda i: (0, 0)
    wsel = lambda i: (layer, 0, 0)
    return pl.pallas_call(
        functools.partial(_ffn_body, final=final),
        grid=(n // FFN_ROWS,),
        in_specs=[
            pl.BlockSpec((FFN_ROWS, D_MODEL), lambda i: (i, 0)),
            pl.BlockSpec((None, 1, D_MODEL), wsel),
            pl.BlockSpec((None, D_MODEL, D_FF), wsel),
            pl.BlockSpec((None, D_MODEL, D_FF), wsel),
            pl.BlockSpec((None, D_FF, D_MODEL), wsel),
            pl.BlockSpec((1, D_MODEL), const),
        ],
        out_specs=pl.BlockSpec((FFN_ROWS, D_MODEL), lambda i: (i, 0)),
        out_shape=jax.ShapeDtypeStruct((n, D_MODEL), F32),
        scratch_shapes=[pltpu.VMEM((FFN_ROWS, D_FF), BF16)],
        compiler_params=pltpu.CompilerParams(
            dimension_semantics=("arbitrary",), vmem_limit_bytes=VMEM_LIMIT),
        name="ffn_half_step",
    )(x2d, nw, wg, wu, wd, fnw)


C_Z, C_Q, C_F, C_I, C_G, C_DT = 0, 512, 1024, 1536, 2048, 2560
PROJ_W = 3072
PROJ_PIECES = tuple((c, c + 256) for c in range(0, D_PROJ, 256))
N_PROJ_TILES = len(PROJ_PIECES)


def _project_tile(i, hn_ref, win_ref, pout, xcout):
    t = MIX_ROWS
    c0, c1 = PROJ_PIECES[i]
    res = _dot(hn_ref[...], win_ref[:, c0:c1])
    if c1 <= SSD_W:
        pout[:, c0:c1] = res
    elif c1 <= SSD_W + CONV_DIM:
        xcout[8:8 + t, c0 - SSD_W:c1 - SSD_W] = res
    else:
        pout[:, c0 - CONV_DIM:c1 - CONV_DIM] = res


def _block_stages(first, lb, a_neg, pin, xcin, y_ref, xres_ref, out_ref, refs):
    (cw_ref, cb_ref, dtb_ref, dsk_ref, snw_ref, hnw_ref, wout_ref, l64_ref, l16_ref, slm_ref, causal_ref,
     code_ref, bd4_ref, gmask_ref, tail_ref, act_ref, dt_ref, acum_ref, seg_ref, q_ref, k_ref, b_ref,
     sst_ref, st_ref) = refs
    t = MIX_ROWS
    nwin = t // WIN
    nch = WIN // HG_CHUNK

    xcin[0:8, :] = jnp.where(first, 0.0, tail_ref[...])
    acc = cb_ref[...] + cw_ref[0:1, :] * xcin[5:5 + t, :]
    for kk in range(1, CONV_K):
        acc = acc + cw_ref[kk:kk + 1, :] * xcin[5 + kk:5 + kk + t, :]
    act_ref[...] = _silu(acc)
    tail_ref[...] = xcin[t:t + 8, :]
    yield

    draw = pin[:, C_DT:C_DT + SSD_W] + dtb_ref[...]
    dtv = jnp.maximum(draw, 0.0) + jnp.log(1.0 + jnp.exp(-jnp.abs(draw)))
    dt_ref[...] = dtv
    d_a = dtv * a_neg
    acum_ref[...] = _dot_exact_lhs(l64_ref[...], d_a)
    seg_ref[...] = _dot_exact_lhs(l64_ref[...], d_a * slm_ref[...])
    yield

    fraw = pin[:, C_F:C_F + HG_W]
    logf = jnp.log(lb + (1.0 - lb) * jax.nn.sigmoid(fraw))
    k_ref[...] = (1.0 - lb) * jax.nn.sigmoid(-fraw)
    q_ref[...] = _silu(pin[:, C_Q:C_Q + HG_W])
    b_ref[...] = _dot_exact_lhs(l16_ref[...], logf)
    yield

    bd4 = bd4_ref[...]
    for w in range(nwin):
        r0 = w * WIN
        rows = slice(r0, r0 + WIN)
        restart = first if w == 0 else None

        xs = act_ref[rows, 0:SSD_W]
        bm = act_ref[rows, SSD_W:SSD_W + SSD_GN].astype(BF16)
        cm = act_ref[rows, SSD_W + SSD_GN:CONV_DIM].astype(BF16)
        ac = acum_ref[rows, :]
        a_last = acum_ref[r0 + WIN - 1:r0 + WIN, :]
        xdt = xs * dt_ref[rows, :]
        xdd = (xdt * jnp.exp(a_last - ac)).astype(BF16)
        xdt = xdt.astype(BF16)
        lane_gn = lax.broadcasted_iota(jnp.int32, (WIN, SSD_GN), 1)
        bm0 = jnp.where(lane_gn < HEAD, bm, 0.0)
        bm1 = jnp.where(lane_gn >= HEAD, bm, 0.0)
        bm_bd = jnp.concatenate([bm0] * 4 + [bm1] * 4, axis=0)
        cbw = lax.dot_general(cm, bm_bd, NT_DIMS, preferred_element_type=F32)
        yield
        mw = (cbw * jnp.exp(seg_ref[rows, :]) * causal_ref[...]).astype(BF16)
        y_parts = []
        for hg in range(2):
            cs = slice(hg * PACK, (hg + 1) * PACK)
            x_bd = jnp.concatenate([xdt[:, cs]] * 4, axis=0) * bd4
            y_parts.append(_dot(mw[:, cs], x_bd))
        y = jnp.concatenate(y_parts, axis=1)
        s_prev = sst_ref[...]
        if restart is not None:
            s_prev = jnp.where(restart, 0.0, s_prev)
        y = y + _dot(cm, (s_prev * gmask_ref[...]).astype(BF16)) * jnp.exp(ac)
        contrib = lax.dot_general(bm, xdd, TN_DIMS, preferred_element_type=F32)
        sst_ref[...] = s_prev * jnp.exp(a_last) + contrib
        yield
        y = y + dsk_ref[...] * xs
        y = y * _silu(pin[rows, C_Z:C_Z + SSD_W])
        y2 = y * y
        ms0 = jnp.mean(y2[:, 0:PACK], axis=-1, keepdims=True)
        ms1 = jnp.mean(y2[:, PACK:2 * PACK], axis=-1, keepdims=True)
        inv = jnp.concatenate([jnp.broadcast_to(lax.rsqrt(ms0 + NORM_EPS), (WIN, PACK)),
                               jnp.broadcast_to(lax.rsqrt(ms1 + NORM_EPS), (WIN, PACK))], axis=1)
        y_ref[rows, 0:SSD_W] = (y * inv * snw_ref[...]).astype(BF16)
        yield

        code = code_ref[...]
        for hg in range(2):
            cs = slice(hg * PACK, (hg + 1) * PACK)
            q = q_ref[rows, cs]
            k = k_ref[rows, cs]
            b = b_ref[rows, cs]
            v = pin[rows, C_I + hg * PACK:C_I + (hg + 1) * PACK].astype(BF16)
            last = [r0 + HG_CHUNK * c + HG_CHUNK - 1 for c in range(nch)]
            mid = [r0 + HG_CHUNK * c + HG_CHUNK // 2 - 1 for c in range(nch)]
            bl = [b_ref[i:i + 1, cs] for i in last]
            br = [b_ref[i:i + 1, cs] for i in mid]
            zero = jnp.zeros_like(bl[0])
            gpre = [zero, bl[0], bl[0] + bl[1], bl[0] + bl[1] + bl[2]]
            gtot = gpre[3] + bl[3]
            e2 = [zero, bl[0], bl[1], bl[2]]
            e3 = [zero, zero, bl[1] + bl[0], bl[2] + bl[1]]

            def bc(rows_list):
                return jnp.concatenate([jnp.broadcast_to(r, (HG_CHUNK, PACK)) for r in rows_list], axis=0)

            bl_bc, br_bc, g_bc = bc(bl), bc(br), bc(gpre)
            qe = (q * jnp.exp(b - br_bc)).astype(BF16)
            ke = (k * jnp.exp(br_bc - b)).astype(BF16)
            kd = k * jnp.exp(bl_bc - b)
            qb = q * jnp.exp(b)
            q2 = qb * jnp.exp(bc(e2))
            q3 = qb * jnp.exp(bc(e3))
            qg = (qb * jnp.exp(g_bc)).astype(BF16)
            k64 = (kd * jnp.exp(gtot - g_bc - bl_bc)).astype(BF16)
            ke_bd = jnp.concatenate([ke] * 4, axis=0) * bd4
            kd_bd = jnp.concatenate([kd.astype(BF16)] * 4, axis=0) * bd4
            v_bd = jnp.concatenate([v] * 4, axis=0) * bd4
            s0 = lax.dot_general(qe, ke_bd, NT_DIMS, preferred_element_type=F32)
            qs = jnp.concatenate([qb, q2, q3], axis=0).astype(BF16)
            s123 = lax.dot_general(qs, kd_bd, NT_DIMS, preferred_element_type=F32)
            yield
            sc = jnp.where(code == 0, s0,
                           jnp.where(code == 1, s123[0:WIN],
                                     jnp.where(code == 2, s123[WIN:2 * WIN],
                                               jnp.where(code == 3, s123[2 * WIN:3 * WIN], 0.0))))
            o = _dot(sc.astype(BF16), v_bd)
            st = st_ref[hg]
            if restart is not None:
                st = jnp.where(restart, 0.0, st)
            o = o + lax.dot_general(qg, st.astype(BF16) * bd4, NT_DIMS, preferred_element_type=F32)
            contrib = lax.dot_general(v, k64, TN_DIMS, preferred_element_type=F32)
            st_ref[hg] = st * jnp.exp(gtot) + contrib
            oo = o * o
            head_of_lane = lax.broadcasted_iota(jnp.int32, (WIN, PACK), 1) // HEAD
            inv = jnp.zeros((WIN, PACK), F32)
            for h in range(PACK // HEAD):
                in_head = head_of_lane == h
                ms = jnp.sum(jnp.where(in_head, oo, 0.0), axis=-1, keepdims=True) * (1.0 / HEAD)
                inv = jnp.where(in_head, lax.rsqrt(ms + NORM_EPS), inv)
            gate = _silu(pin[rows, C_G + hg * PACK:C_G + (hg + 1) * PACK])
            yh = o * inv * hnw_ref[:, cs] * gate
            y_ref[rows, SSD_W + hg * PACK:SSD_W + (hg + 1) * PACK] = yh.astype(BF16)
            yield

        if w % 2 == 1:
            orow = slice(r0 - WIN, r0 + WIN)
            out_ref[orow, :] = xres_ref[orow, :] + _dot(y_ref[orow, :], wout_ref[...])
            yield


def _mixer_body(xb_ref, xa0_ref, xa1_ref, nw_ref, win_ref, cw_ref, cb_ref, dtb_ref, alog_ref, dsk_ref,
                snw_ref, lbl_ref, hnw_ref, wout_ref, l64_ref, l16_ref, slm_ref, causal_ref, code_ref,
                bd4_ref, gmask_ref,
                o_ref,
                p0_ref, p1_ref, xc0_ref, xc1_ref, hn_ref, tail_ref, act_ref, dt_ref, acum_ref, seg_ref,
                q_ref, k_ref, b_ref, y0_ref, y1_ref, sst_ref, st_ref, *, layer, blocks_per_seq):
    t = MIX_ROWS
    u = pl.program_id(0)

    @pl.when(u == 0)
    def _():
        tail_ref[...] = jnp.zeros_like(tail_ref)
        sst_ref[...] = jnp.zeros_like(sst_ref)
        st_ref[...] = jnp.zeros_like(st_ref)
        hn_ref[...] = _rms(xb_ref[0:t, :], nw_ref[...]).astype(BF16)
        for i in range(N_PROJ_TILES):
            _project_tile(i, hn_ref, win_ref, p0_ref, xc0_ref)

    p = jax.nn.softmax(lbl_ref[...], axis=0)
    lb = jnp.zeros((1, HG_W), F32)
    for j in range(1, layer + 1):
        lb = lb + p[j:j + 1, :]
    a_neg = -jnp.exp(alog_ref[...])
    refs = (cw_ref, cb_ref, dtb_ref, dsk_ref, snw_ref, hnw_ref, wout_ref, l64_ref, l16_ref, slm_ref, causal_ref,
            code_ref, bd4_ref, gmask_ref, tail_ref, act_ref, dt_ref, acum_ref, seg_ref, q_ref, k_ref, b_ref,
            sst_ref, st_ref)

    for half in range(2):
        pin, xcin = (p0_ref, xc0_ref) if half == 0 else (p1_ref, xc1_ref)
        pout, xcout = (p1_ref, xc1_ref) if half == 0 else (p0_ref, xc0_ref)
        xa_ref = xa0_ref if half == 0 else xa1_ref
        y_ref = y0_ref if half == 0 else y1_ref
        first = lax.rem(2 * u + half, blocks_per_seq) == 0
        brow = slice(half * t, (half + 1) * t)

        hn_ref[...] = _rms(xa_ref[...], nw_ref[...]).astype(BF16)
        stages = _block_stages(first, lb, a_neg, pin, xcin, y_ref, xb_ref.at[brow, :], o_ref.at[brow, :], refs)
        tile = 0
        for n_stage, _ in enumerate(stages):
            if n_stage % 2 == 0 and tile < N_PROJ_TILES:
                _project_tile(tile, hn_ref, win_ref, pout, xcout)
                tile += 1
        while tile < N_PROJ_TILES:
            _project_tile(tile, hn_ref, win_ref, pout, xcout)
            tile += 1


def _mixer_consts():
    t = MIX_ROWS
    r = np.arange(t)
    same64 = (r[:, None] // WIN) == (r[None, :] // WIN)
    same16 = (r[:, None] // HG_CHUNK) == (r[None, :] // HG_CHUNK)
    lower = r[None, :] <= r[:, None]
    l64 = (same64 & lower).astype(np.float32)
    l16 = (same16 & lower).astype(np.float32)
    l64 = np.concatenate([l64, l64], axis=1)
    l16 = np.concatenate([l16, l16], axis=1)
    lane = np.arange(SSD_W)
    slm = ((r[:, None] % WIN) > (lane[None, :] % HEAD)).astype(np.float32)
    tt = np.arange(WIN)
    causal = (tt[:, None] >= (lane[None, :] % HEAD)).astype(np.float32)
    ls = np.arange(PACK) % HEAD
    dch = tt[:, None] // HG_CHUNK - ls[None, :] // HG_CHUNK
    code = np.where(dch == 0, np.where(ls[None, :] <= tt[:, None], 0, 9), np.where((dch >= 1) & (dch <= 3), dch, 9))
    rp = np.arange(PACK)
    bd4 = ((rp[:, None] // HEAD) == (rp[None, :] // HEAD)).astype(np.float32)
    gn = np.arange(SSD_GN)
    gmask = ((gn[:, None] // HEAD) == (lane[None, :] // PACK)).astype(np.float32)
    return (jnp.asarray(l64, BF16), jnp.asarray(l16, BF16), jnp.asarray(slm, F32), jnp.asarray(causal, F32),
            jnp.asarray(code.astype(np.float32), F32), jnp.asarray(bd4, BF16), jnp.asarray(gmask, F32))


def _mixer(x2d, nw, win, cw, cb, dtb, alog, dsk, snw, lbl, hnw, wout, consts, layer, seq):
    n = x2d.shape[0]
    t = MIX_ROWS
    nblk = n // t
    const2 = lambda u: (0, 0)
    lsel = lambda u: (layer, 0, 0)
    full = lambda a: pl.BlockSpec(a.shape, const2)
    l64, l16, slm, causal, code, bd4, gmask = consts
    return pl.pallas_call(
        functools.partial(_mixer_body, layer=layer, blocks_per_seq=seq // t),
        grid=(nblk // 2,),
        in_specs=[
            pl.BlockSpec((2 * t, D_MODEL), lambda u: (u, 0)),
            pl.BlockSpec((t, D_MODEL), lambda u: (2 * u + 1, 0)),
            pl.BlockSpec((t, D_MODEL), lambda u: (jnp.minimum(2 * u + 2, nblk - 1), 0)),
            pl.BlockSpec((None, 1, D_MODEL), lsel),
            pl.BlockSpec((None, D_MODEL, D_PROJ), lsel),
            pl.BlockSpec((None, CONV_K, CONV_DIM), lsel),
            pl.BlockSpec((None, 1, CONV_DIM), lsel),
            pl.BlockSpec((None, 1, SSD_W), lsel),
            pl.BlockSpec((None, 1, SSD_W), lsel),
            pl.BlockSpec((None, 1, SSD_W), lsel),
            pl.BlockSpec((None, 1, SSD_W), lsel),
            full(lbl),
            pl.BlockSpec((None, 1, HG_W), lsel),
            pl.BlockSpec((None, D_MODEL, D_MODEL), lsel),
            full(l64), full(l16), full(slm), full(causal), full(code), full(bd4), full(gmask),
        ],
        out_specs=pl.BlockSpec((2 * t, D_MODEL), lambda u: (u, 0)),
        out_shape=jax.ShapeDtypeStruct(x2d.shape, F32),
        scratch_shapes=[
            pltpu.VMEM((t, PROJ_W), F32),
            pltpu.VMEM((t, PROJ_W), F32),
            pltpu.VMEM((t + 8, CONV_DIM), F32),
            pltpu.VMEM((t + 8, CONV_DIM), F32),
            pltpu.VMEM((t, D_MODEL), BF16),
            pltpu.VMEM((8, CONV_DIM), F32),
            pltpu.VMEM((t, CONV_DIM), F32),
            pltpu.VMEM((t, SSD_W), F32),
            pltpu.VMEM((t, SSD_W), F32),
            pltpu.VMEM((t, SSD_W), F32),
            pltpu.VMEM((t, HG_W), F32),
            pltpu.VMEM((t, HG_W), F32),
            pltpu.VMEM((t, HG_W), F32),
            pltpu.VMEM((t, D_MODEL), BF16),
            pltpu.VMEM((t, D_MODEL), BF16),
            pltpu.VMEM((SSD_GN, SSD_W), F32),
            pltpu.VMEM((2, PACK, PACK), F32),
        ],
        compiler_params=pltpu.CompilerParams(
            dimension_semantics=("arbitrary",), vmem_limit_bytes=VMEM_LIMIT),
        name="token_mixer",
    )(x2d, x2d, x2d, nw, win, cw, cb, dtb, alog, dsk, snw, lbl, hnw, wout, l64, l16, slm, causal, code, bd4, gmask)


def kernel(x, ffn1_norm, ffn1_w_gate, ffn1_w_up, ffn1_w_down, mix_norm, w_in, ssd_conv_w, ssd_conv_b,
           ssd_dt_bias, ssd_a_log, ssd_d, ssd_norm_w, hg_lb_logits, hg_norm_w, w_out, ffn2_norm,
           ffn2_w_gate, ffn2_w_up, ffn2_w_down, final_norm):
    bsz, seq, _ = x.shape
    depth = w_in.shape[0]
    rep = lambda a: jnp.repeat(a.astype(F32), HEAD, axis=-1)[:, None, :]
    row = lambda a: a.astype(F32)[:, None, :]
    w_dt = jnp.repeat(w_in[:, :, 1280:1288], HEAD, axis=-1)
    win_k = jnp.concatenate([w_in[:, :, 0:1280], w_in[:, :, 1288:D_PROJ_IN], w_dt], axis=-1).astype(BF16)
    wout_k = w_out.astype(BF16)
    f1 = (row(ffn1_norm), ffn1_w_gate.astype(BF16), ffn1_w_up.astype(BF16), ffn1_w_down.astype(BF16))
    f2 = (row(ffn2_norm), ffn2_w_gate.astype(BF16), ffn2_w_up.astype(BF16), ffn2_w_down.astype(BF16))
    fnw = final_norm.astype(F32)[None, :]
    hnw = jnp.tile(hg_norm_w.astype(F32), (1, HG_W // HEAD))[:, None, :]
    consts = _mixer_consts()

    xf = x.astype(F32).reshape(bsz * seq, D_MODEL)
    for l in range(depth):
        xf = _ffn(xf, *f1, fnw, l, False)
        xf = _mixer(xf, row(mix_norm), win_k, ssd_conv_w.astype(F32), row(ssd_conv_b), rep(ssd_dt_bias),
                    rep(ssd_a_log), rep(ssd_d), row(ssd_norm_w), hg_lb_logits.astype(F32), hnw, wout_k,
                    consts, l, seq)
        xf = _ffn(xf, *f2, fnw, l, l == depth - 1)
    return xf.reshape(bsz, seq, D_MODEL).astype(x.dtype)
```

```python
import functools

import numpy as np
import jax
import jax.numpy as jnp
from jax import lax
from jax.experimental import pallas as pl
from jax.experimental.pallas import tpu as pltpu

F32 = jnp.float32
BF16 = jnp.bfloat16

D_MODEL = 1024
D_FF = 2816
SSD_W = 512
SSD_GN = 128
CONV_DIM = SSD_W + 2 * SSD_GN
CONV_K = 4
HG_W = 512
HEAD = 64
WIN = 64
HG_CHUNK = 16
PACK = 256
NORM_EPS = 1e-6

D_PROJ_IN = 3336
D_PROJ = 3840

FFN_ROWS = 512
FFN_COLS = 256
MIX_ROWS = 256
VMEM_LIMIT = 56 * 1024 * 1024

NT_DIMS = (((1,), (1,)), ((), ()))
TN_DIMS = (((0,), (0,)), ((), ()))


def _rms(x, w):
    return x * lax.rsqrt(jnp.mean(x * x, axis=-1, keepdims=True) + NORM_EPS) * w


def _silu(x):
    return x * jax.nn.sigmoid(x)


def _dot(a, b):
    return jnp.dot(a, b, preferred_element_type=F32)


def _dot_exact_lhs(mm, x):
    hi = x.astype(BF16)
    lo = (x - hi.astype(F32)).astype(BF16)
    return _dot(mm, jnp.concatenate([hi, lo], axis=0))


def _ffn_body(x_ref, nw_ref, wg_ref, wu_ref, wd_ref, fnw_ref, o_ref, h_ref, *, final):
    half = FFN_ROWS // 2
    for blk in range(2):
        rows = slice(blk * half, (blk + 1) * half)
        x = x_ref[rows, :]
        xn = _rms(x, nw_ref[...]).astype(BF16)
        for c in range(D_FF // FFN_COLS):
            cols = slice(c * FFN_COLS, (c + 1) * FFN_COLS)
            g = _dot(xn, wg_ref[:, cols])
            u = _dot(xn, wu_ref[:, cols])
            h_ref[rows, cols] = (_silu(g) * u).astype(BF16)
        y = x + 0.5 * _dot(h_ref[rows, :], wd_ref[...])
        if final:
            y = _rms(y, fnw_ref[...])
        o_ref[rows, :] = y


def _ffn(x2d, nw, wg, wu, wd, fnw, layer, final):
    n = x2d.shape[0]
    const = lambda i: (0, 0)
    wsel = lambda i: (layer, 0, 0)
    return pl.pallas_call(
        functools.partial(_ffn_body, final=final),
        grid=(n // FFN_ROWS,),
        in_specs=[
            pl.BlockSpec((FFN_ROWS, D_MODEL), lambda i: (i, 0)),
            pl.BlockSpec((None, 1, D_MODEL), wsel),
            pl.BlockSpec((None, D_MODEL, D_FF), wsel),
            pl.BlockSpec((None, D_MODEL, D_FF), wsel),
            pl.BlockSpec((None, D_FF, D_MODEL), wsel),
            pl.BlockSpec((1, D_MODEL), const),
        ],
        out_specs=pl.BlockSpec((FFN_ROWS, D_MODEL), lambda i: (i, 0)),
        out_shape=jax.ShapeDtypeStruct((n, D_MODEL), F32),
        scratch_shapes=[pltpu.VMEM((FFN_ROWS, D_FF), BF16)],
        compiler_params=pltpu.CompilerParams(
            dimension_semantics=("arbitrary",), vmem_limit_bytes=VMEM_LIMIT),
        name="ffn_half_step",
    )(x2d, nw, wg, wu, wd, fnw)


C_Z, C_Q, C_F, C_I, C_G, C_DT = 0, 512, 1024, 1536, 2048, 2560
PROJ_W = 3072
PROJ_PIECES = tuple((c, c + 256) for c in range(0, D_PROJ, 256))
N_PROJ_TILES = len(PROJ_PIECES)


def _project_tile(i, hn_ref, win_ref, pout, xcout):
    t = MIX_ROWS
    c0, c1 = PROJ_PIECES[i]
    res = _dot(hn_ref[...], win_ref[:, c0:c1])
    if c1 <= SSD_W:
        pout[:, c0:c1] = res
    elif c1 <= SSD_W + CONV_DIM:
        xcout[8:8 + t, c0 - SSD_W:c1 - SSD_W] = res
    else:
        pout[:, c0 - CONV_DIM:c1 - CONV_DIM] = res


def _prep_block(first, lb, a_neg, pin, xcin, refs):
    (cw_ref, cb_ref, dtb_ref, dsk_ref, snw_ref, hnw_ref, wout_ref, l64_ref, l16_ref, slm_ref, causal_ref,
     code_ref, bd4_ref, gmask_ref, tail_ref, act_ref, dt_ref, acum_ref, seg_ref, q_ref, k_ref, b_ref,
     sst_ref, st_ref) = refs
    t = MIX_ROWS

    xcin[0:8, :] = jnp.where(first, 0.0, tail_ref[...])
    acc = cb_ref[...] + cw_ref[0:1, :] * xcin[5:5 + t, :]
    for kk in range(1, CONV_K):
        acc = acc + cw_ref[kk:kk + 1, :] * xcin[5 + kk:5 + kk + t, :]
    act_ref[...] = _silu(acc)
    tail_ref[...] = xcin[t:t + 8, :]

    draw = pin[:, C_DT:C_DT + SSD_W] + dtb_ref[...]
    dtv = jnp.maximum(draw, 0.0) + jnp.log(1.0 + jnp.exp(-jnp.abs(draw)))
    dt_ref[...] = dtv
    d_a = dtv * a_neg
    acum_ref[...] = _dot_exact_lhs(l64_ref[...], d_a)
    seg_ref[...] = _dot_exact_lhs(l64_ref[...], d_a * slm_ref[...])

    fraw = pin[:, C_F:C_F + HG_W]
    logf = jnp.log(lb + (1.0 - lb) * jax.nn.sigmoid(fraw))
    k_ref[...] = (1.0 - lb) * jax.nn.sigmoid(-fraw)
    q_ref[...] = _silu(pin[:, C_Q:C_Q + HG_W])
    b_ref[...] = _dot_exact_lhs(l16_ref[...], logf)


def _window_stages(first, pin, y_ref, refs):
    (cw_ref, cb_ref, dtb_ref, dsk_ref, snw_ref, hnw_ref, wout_ref, l64_ref, l16_ref, slm_ref, causal_ref,
     code_ref, bd4_ref, gmask_ref, tail_ref, act_ref, dt_ref, acum_ref, seg_ref, q_ref, k_ref, b_ref,
     sst_ref, st_ref) = refs
    nwin = MIX_ROWS // WIN
    nch = WIN // HG_CHUNK
    bd4 = bd4_ref[...]
    for w in range(nwin):
        r0 = w * WIN
        rows = slice(r0, r0 + WIN)
        restart = first if w == 0 else None

        xs = act_ref[rows, 0:SSD_W]
        bm = act_ref[rows, SSD_W:SSD_W + SSD_GN].astype(BF16)
        cm = act_ref[rows, SSD_W + SSD_GN:CONV_DIM].astype(BF16)
        ac = acum_ref[rows, :]
        a_last = acum_ref[r0 + WIN - 1:r0 + WIN, :]
        xdt = xs * dt_ref[rows, :]
        xdd = (xdt * jnp.exp(a_last - ac)).astype(BF16)
        xdt = xdt.astype(BF16)
        lane_gn = lax.broadcasted_iota(jnp.int32, (WIN, SSD_GN), 1)
        bm0 = jnp.where(lane_gn < HEAD, bm, 0.0)
        bm1 = jnp.where(lane_gn >= HEAD, bm, 0.0)
        bm_bd = jnp.concatenate([bm0] * 4 + [bm1] * 4, axis=0)
        cbw = lax.dot_general(cm, bm_bd, NT_DIMS, preferred_element_type=F32)
        yield
        mw = (cbw * jnp.exp(seg_ref[rows, :]) * causal_ref[...]).astype(BF16)
        y_parts = []
        for hg in range(2):
            cs = slice(hg * PACK, (hg + 1) * PACK)
            x_bd = jnp.concatenate([xdt[:, cs]] * 4, axis=0) * bd4
            y_parts.append(_dot(mw[:, cs], x_bd))
        y = jnp.concatenate(y_parts, axis=1)
        s_prev = sst_ref[...]
        if restart is not None:
            s_prev = jnp.where(restart, 0.0, s_prev)
        y = y + _dot(cm, (s_prev * gmask_ref[...]).astype(BF16)) * jnp.exp(ac)
        contrib = lax.dot_general(bm, xdd, TN_DIMS, preferred_element_type=F32)
        sst_ref[...] = s_prev * jnp.exp(a_last) + contrib
        yield
        y = y + dsk_ref[...] * xs
        y = y * _silu(pin[rows, C_Z:C_Z + SSD_W])
        y2 = y * y
        ms0 = jnp.mean(y2[:, 0:PACK], axis=-1, keepdims=True)
        ms1 = jnp.mean(y2[:, PACK:2 * PACK], axis=-1, keepdims=True)
        inv = jnp.concatenate([jnp.broadcast_to(lax.rsqrt(ms0 + NORM_EPS), (WIN, PACK)),
                               jnp.broadcast_to(lax.rsqrt(ms1 + NORM_EPS), (WIN, PACK))], axis=1)
        y_ref[rows, 0:SSD_W] = (y * inv * snw_ref[...]).astype(BF16)
        yield

        code = code_ref[...]
        for hg in range(2):
            cs = slice(hg * PACK, (hg + 1) * PACK)
            q = q_ref[rows, cs]
            k = k_ref[rows, cs]
            b = b_ref[rows, cs]
            v = pin[rows, C_I + hg * PACK:C_I + (hg + 1) * PACK].astype(BF16)
            last = [r0 + HG_CHUNK * c + HG_CHUNK - 1 for c in range(nch)]
            mid = [r0 + HG_CHUNK * c + HG_CHUNK // 2 - 1 for c in range(nch)]
            bl = [b_ref[i:i + 1, cs] for i in last]
            br = [b_ref[i:i + 1, cs] for i in mid]
            zero = jnp.zeros_like(bl[0])
            gpre = [zero, bl[0], bl[0] + bl[1], bl[0] + bl[1] + bl[2]]
            gtot = gpre[3] + bl[3]
            cr = [bl[c] - br[c] for c in range(nch)]
            e1 = [zero, cr[0], cr[1], cr[2]]
            e2 = [zero, zero, bl[1] + cr[0], bl[2] + cr[1]]
            e3 = [zero, zero, zero, bl[2] + bl[1] + cr[0]]

            def bc(rows_list):
                return jnp.concatenate([jnp.broadcast_to(r, (HG_CHUNK, PACK)) for r in rows_list], axis=0)

            br_bc, g_bc = bc(br), bc(gpre)
            qe = q * jnp.exp(b - br_bc)
            ke = (k * jnp.exp(br_bc - b)).astype(BF16)
            qb = q * jnp.exp(b)
            q1 = qb * jnp.exp(bc(e1))
            q2 = qb * jnp.exp(bc(e2))
            q3 = qb * jnp.exp(bc(e3))
            qg = (qb * jnp.exp(g_bc)).astype(BF16)
            k64 = (k * jnp.exp(gtot - g_bc - b)).astype(BF16)
            ke_bd = jnp.concatenate([ke] * 4, axis=0) * bd4
            v_bd = jnp.concatenate([v] * 4, axis=0) * bd4
            qs = jnp.concatenate([qe, q1, q2, q3], axis=0).astype(BF16)
            s_all = lax.dot_general(qs, ke_bd, NT_DIMS, preferred_element_type=F32)
            yield
            sc = jnp.where(code == 0, s_all[0:WIN],
                           jnp.where(code == 1, s_all[WIN:2 * WIN],
                                     jnp.where(code == 2, s_all[2 * WIN:3 * WIN],
                                               jnp.where(code == 3, s_all[3 * WIN:4 * WIN], 0.0))))
            o = _dot(sc.astype(BF16), v_bd)
            st = st_ref[hg]
            if restart is not None:
                st = jnp.where(restart, 0.0, st)
            o = o + lax.dot_general(qg, st.astype(BF16) * bd4, NT_DIMS, preferred_element_type=F32)
            contrib = lax.dot_general(v, k64, TN_DIMS, preferred_element_type=F32)
            st_ref[hg] = st * jnp.exp(gtot) + contrib
            oo = o * o
            head_of_lane = lax.broadcasted_iota(jnp.int32, (WIN, PACK), 1) // HEAD
            inv = jnp.zeros((WIN, PACK), F32)
            for h in range(PACK // HEAD):
                in_head = head_of_lane == h
                ms = jnp.sum(jnp.where(in_head, oo, 0.0), axis=-1, keepdims=True) * (1.0 / HEAD)
                inv = jnp.where(in_head, lax.rsqrt(ms + NORM_EPS), inv)
            gate = _silu(pin[rows, C_G + hg * PACK:C_G + (hg + 1) * PACK])
            yh = o * inv * hnw_ref[:, cs] * gate
            y_ref[rows, SSD_W + hg * PACK:SSD_W + (hg + 1) * PACK] = yh.astype(BF16)
            yield


def _mixer_body(xb_ref, xa0_ref, xa1_ref, nw_ref, win_ref, cw_ref, cb_ref, dtb_ref, alog_ref, dsk_ref,
                snw_ref, lbl_ref, hnw_ref, wout_ref, l64_ref, l16_ref, slm_ref, causal_ref, code_ref,
                bd4_ref, gmask_ref,
                o_ref,
                p0_ref, p1_ref, xc0_ref, xc1_ref, hn_ref, tail_ref, act_ref, dt_ref, acum_ref, seg_ref,
                q_ref, k_ref, b_ref, y0_ref, y1_ref, sst_ref, st_ref, *, layer, blocks_per_seq):
    t = MIX_ROWS
    u = pl.program_id(0)

    refs = (cw_ref, cb_ref, dtb_ref, dsk_ref, snw_ref, hnw_ref, wout_ref, l64_ref, l16_ref, slm_ref, causal_ref,
            code_ref, bd4_ref, gmask_ref, tail_ref, act_ref, dt_ref, acum_ref, seg_ref, q_ref, k_ref, b_ref,
            sst_ref, st_ref)

    def layer_consts():
        p = jax.nn.softmax(lbl_ref[...], axis=0)
        lb = jnp.zeros((1, HG_W), F32)
        for j in range(1, layer + 1):
            lb = lb + p[j:j + 1, :]
        return lb, -jnp.exp(alog_ref[...])

    @pl.when(u == 0)
    def _():
        tail_ref[...] = jnp.zeros_like(tail_ref)
        sst_ref[...] = jnp.zeros_like(sst_ref)
        st_ref[...] = jnp.zeros_like(st_ref)
        hn_ref[...] = _rms(xb_ref[0:t, :], nw_ref[...]).astype(BF16)
        for i in range(N_PROJ_TILES):
            _project_tile(i, hn_ref, win_ref, p0_ref, xc0_ref)
        _prep_block(True, *layer_consts(), p0_ref, xc0_ref, refs)

    lb, a_neg = layer_consts()
    for half in range(2):
        pin = p0_ref if half == 0 else p1_ref
        pout, xcout = (p1_ref, xc1_ref) if half == 0 else (p0_ref, xc0_ref)
        xa_ref = xa0_ref if half == 0 else xa1_ref
        y_ref = y0_ref if half == 0 else y1_ref
        first = lax.rem(2 * u + half, blocks_per_seq) == 0
        first_next = lax.rem(2 * u + half + 1, blocks_per_seq) == 0
        brow = slice(half * t, (half + 1) * t)

        hn_ref[...] = _rms(xa_ref[...], nw_ref[...]).astype(BF16)
        tile = 0
        for n_stage, _ in enumerate(_window_stages(first, pin, y_ref, refs)):
            if n_stage % 2 == 0 and tile < N_PROJ_TILES:
                _project_tile(tile, hn_ref, win_ref, pout, xcout)
                tile += 1
        while tile < N_PROJ_TILES:
            _project_tile(tile, hn_ref, win_ref, pout, xcout)
            tile += 1
        o_ref[brow, :] = xb_ref[brow, :] + _dot(y_ref[...], wout_ref[...])
        _prep_block(first_next, lb, a_neg, pout, xcout, refs)


def _mixer_consts():
    t = MIX_ROWS
    r = np.arange(t)
    same64 = (r[:, None] // WIN) == (r[None, :] // WIN)
    same16 = (r[:, None] // HG_CHUNK) == (r[None, :] // HG_CHUNK)
    lower = r[None, :] <= r[:, None]
    l64 = (same64 & lower).astype(np.float32)
    l16 = (same16 & lower).astype(np.float32)
    l64 = np.concatenate([l64, l64], axis=1)
    l16 = np.concatenate([l16, l16], axis=1)
    lane = np.arange(SSD_W)
    slm = ((r[:, None] % WIN) > (lane[None, :] % HEAD)).astype(np.float32)
    tt = np.arange(WIN)
    causal = (tt[:, None] >= (lane[None, :] % HEAD)).astype(np.float32)
    ls = np.arange(PACK) % HEAD
    dch = tt[:, None] // HG_CHUNK - ls[None, :] // HG_CHUNK
    code = np.where(dch == 0, np.where(ls[None, :] <= tt[:, None], 0, 9), np.where((dch >= 1) & (dch <= 3), dch, 9))
    rp = np.arange(PACK)
    bd4 = ((rp[:, None] // HEAD) == (rp[None, :] // HEAD)).astype(np.float32)
    gn = np.arange(SSD_GN)
    gmask = ((gn[:, None] // HEAD) == (lane[None, :] // PACK)).astype(np.float32)
    return (jnp.asarray(l64, BF16), jnp.asarray(l16, BF16), jnp.asarray(slm, F32), jnp.asarray(causal, F32),
            jnp.asarray(code.astype(np.float32), F32), jnp.asarray(bd4, BF16), jnp.asarray(gmask, F32))


def _mixer(x2d, nw, win, cw, cb, dtb, alog, dsk, snw, lbl, hnw, wout, consts, layer, seq):
    n = x2d.shape[0]
    t = MIX_ROWS
    nblk = n // t
    const2 = lambda u: (0, 0)
    lsel = lambda u: (layer, 0, 0)
    full = lambda a: pl.BlockSpec(a.shape, const2)
    l64, l16, slm, causal, code, bd4, gmask = consts
    return pl.pallas_call(
        functools.partial(_mixer_body, layer=layer, blocks_per_seq=seq // t),
        grid=(nblk // 2,),
        in_specs=[
            pl.BlockSpec((2 * t, D_MODEL), lambda u: (u, 0)),
            pl.BlockSpec((t, D_MODEL), lambda u: (2 * u + 1, 0)),
            pl.BlockSpec((t, D_MODEL), lambda u: (jnp.minimum(2 * u + 2, nblk - 1), 0)),
            pl.BlockSpec((None, 1, D_MODEL), lsel),
            pl.BlockSpec((None, D_MODEL, D_PROJ), lsel),
            pl.BlockSpec((None, CONV_K, CONV_DIM), lsel),
            pl.BlockSpec((None, 1, CONV_DIM), lsel),
            pl.BlockSpec((None, 1, SSD_W), lsel),
            pl.BlockSpec((None, 1, SSD_W), lsel),
            pl.BlockSpec((None, 1, SSD_W), lsel),
            pl.BlockSpec((None, 1, SSD_W), lsel),
            full(lbl),
            pl.BlockSpec((None, 1, HG_W), lsel),
            pl.BlockSpec((None, D_MODEL, D_MODEL), lsel),
            full(l64), full(l16), full(slm), full(causal), full(code), full(bd4), full(gmask),
        ],
        out_specs=pl.BlockSpec((2 * t, D_MODEL), lambda u: (u, 0)),
        out_shape=jax.ShapeDtypeStruct(x2d.shape, F32),
        scratch_shapes=[
            pltpu.VMEM((t, PROJ_W), F32),
            pltpu.VMEM((t, PROJ_W), F32),
            pltpu.VMEM((t + 8, CONV_DIM), F32),
            pltpu.VMEM((t + 8, CONV_DIM), F32),
            pltpu.VMEM((t, D_MODEL), BF16),
            pltpu.VMEM((8, CONV_DIM), F32),
            pltpu.VMEM((t, CONV_DIM), F32),
            pltpu.VMEM((t, SSD_W), F32),
            pltpu.VMEM((t, SSD_W), F32),
            pltpu.VMEM((t, SSD_W), F32),
            pltpu.VMEM((t, HG_W), F32),
            pltpu.VMEM((t, HG_W), F32),
            pltpu.VMEM((t, HG_W), F32),
            pltpu.VMEM((t, D_MODEL), BF16),
            pltpu.VMEM((t, D_MODEL), BF16),
            pltpu.VMEM((SSD_GN, SSD_W), F32),
            pltpu.VMEM((2, PACK, PACK), F32),
        ],
        compiler_params=pltpu.CompilerParams(
            dimension_semantics=("arbitrary",), vmem_limit_bytes=VMEM_LIMIT),
        name="token_mixer",
    )(x2d, x2d, x2d, nw, win, cw, cb, dtb, alog, dsk, snw, lbl, hnw, wout, l64, l16, slm, causal, code, bd4, gmask)


def kernel(x, ffn1_norm, ffn1_w_gate, ffn1_w_up, ffn1_w_down, mix_norm, w_in, ssd_conv_w, ssd_conv_b,
           ssd_dt_bias, ssd_a_log, ssd_d, ssd_norm_w, hg_lb_logits, hg_norm_w, w_out, ffn2_norm,
           ffn2_w_gate, ffn2_w_up, ffn2_w_down, final_norm):
    bsz, seq, _ = x.shape
    depth = w_in.shape[0]
    rep = lambda a: jnp.repeat(a.astype(F32), HEAD, axis=-1)[:, None, :]
    row = lambda a: a.astype(F32)[:, None, :]
    w_dt = jnp.repeat(w_in[:, :, 1280:1288], HEAD, axis=-1)
    win_k = jnp.concatenate([w_in[:, :, 0:1280], w_in[:, :, 1288:D_PROJ_IN], w_dt], axis=-1).astype(BF16)
    wout_k = w_out.astype(BF16)
    f1 = (row(ffn1_norm), ffn1_w_gate.astype(BF16), ffn1_w_up.astype(BF16), ffn1_w_down.astype(BF16))
    f2 = (row(ffn2_norm), ffn2_w_gate.astype(BF16), ffn2_w_up.astype(BF16), ffn2_w_down.astype(BF16))
    fnw = final_norm.astype(F32)[None, :]
    hnw = jnp.tile(hg_norm_w.astype(F32), (1, HG_W // HEAD))[:, None, :]
    consts = _mixer_consts()

    xf = x.astype(F32).reshape(bsz * seq, D_MODEL)
    for l in range(depth):
        xf = _ffn(xf, *f1, fnw, l, False)
        xf = _mixer(xf, row(mix_norm), win_k, ssd_conv_w.astype(F32), row(ssd_conv_b), rep(ssd_dt_bias),
                    rep(ssd_a_log), rep(ssd_d), row(ssd_norm_w), hg_lb_logits.astype(F32), hnw, wout_k,
                    consts, l, seq)
        xf = _ffn(xf, *f2, fnw, l, l == depth - 1)
    return xf.reshape(bsz, seq, D_MODEL).astype(x.dtype)
```

```python
import functools

import numpy as np
import jax
import jax.numpy as jnp
from jax import lax
from jax.experimental import pallas as pl
from jax.experimental.pallas import tpu as pltpu

F32 = jnp.float32
BF16 = jnp.bfloat16

D_MODEL = 1024
D_FF = 2816
SSD_W = 512
SSD_GN = 128
CONV_DIM = SSD_W + 2 * SSD_GN
CONV_K = 4
CONV_PAD = 8
HG_W = 512
HEAD = 64
WIN = 64
HG_CHUNK = 16
PACK = 256
NORM_EPS = 1e-6
LOG2_E = 1.4426950408889634

D_PROJ_IN = 3336
W_IN_DT0 = SSD_W + CONV_DIM
W_IN_DT1 = W_IN_DT0 + SSD_W // HEAD
D_PROJ = 3840

FFN_ROWS = 512
FFN_COLS = 256
MIX_ROWS = 256
V7X_VMEM_BYTES = 64 * 1024 * 1024
VMEM_LIMIT = V7X_VMEM_BYTES * 7 // 8

NT_DIMS = (((1,), (1,)), ((), ()))
TN_DIMS = (((0,), (0,)), ((), ()))


def _rms(x, w):
    return x * lax.rsqrt(jnp.mean(x * x, axis=-1, keepdims=True) + NORM_EPS) * w


def _silu(x):
    return x * jax.nn.sigmoid(x)


def _dot(a, b):
    return jnp.dot(a, b, preferred_element_type=F32)


def _dot_exact_lhs(mm, x):
    hi = x.astype(BF16)
    lo = (x - hi.astype(F32)).astype(BF16)
    return _dot(mm, jnp.concatenate([hi, lo], axis=0))


def _ffn_body(x_ref, nw_ref, wg_ref, wu_ref, wd_ref, fnw_ref, o_ref, h_ref, *, final):
    half = FFN_ROWS // 2
    for blk in range(2):
        rows = slice(blk * half, (blk + 1) * half)
        x = x_ref[rows, :]
        xn = _rms(x, nw_ref[...]).astype(BF16)
        for c in range(D_FF // FFN_COLS):
            cols = slice(c * FFN_COLS, (c + 1) * FFN_COLS)
            g = _dot(xn, wg_ref[:, cols])
            u = _dot(xn, wu_ref[:, cols])
            h_ref[rows, cols] = (_silu(g) * u).astype(BF16)
        y = x + 0.5 * _dot(h_ref[rows, :], wd_ref[...])
        if final:
            y = _rms(y, fnw_ref[...])
        o_ref[rows, :] = y


def _ffn(x2d, nw, wg, wu, wd, fnw, layer, final):
    n = x2d.shape[0]
    const = lambda i: (0, 0)
    wsel = lambda i: (layer, 0, 0)
    return pl.pallas_call(
        functools.partial(_ffn_body, final=final),
        grid=(n // FFN_ROWS,),
        in_specs=[
            pl.BlockSpec((FFN_ROWS, D_MODEL), lambda i: (i, 0)),
            pl.BlockSpec((None, 1, D_MODEL), wsel),
            pl.BlockSpec((None, D_MODEL, D_FF), wsel),
            pl.BlockSpec((None, D_MODEL, D_FF), wsel),
            pl.BlockSpec((None, D_FF, D_MODEL), wsel),
            pl.BlockSpec((1, D_MODEL), const),
        ],
        out_specs=pl.BlockSpec((FFN_ROWS, D_MODEL), lambda i: (i, 0)),
        out_shape=jax.ShapeDtypeStruct((n, D_MODEL), F32),
        scratch_shapes=[pltpu.VMEM((FFN_ROWS, D_FF), BF16)],
        compiler_params=pltpu.CompilerParams(
            dimension_semantics=("arbitrary",), vmem_limit_bytes=VMEM_LIMIT),
        name="ffn_half_step",
    )(x2d, nw, wg, wu, wd, fnw)


C_Z, C_Q, C_F, C_I, C_G, C_DT = 0, 512, 1024, 1536, 2048, 2560
PROJ_W = 3072
PROJ_PIECES = tuple((c, c + 256) for c in range(0, D_PROJ, 256))
N_PROJ_TILES = len(PROJ_PIECES)
N_EARLY_TILES = 3


def _project_tile(i, hn_ref, win_ref, pout, xcout):
    t = MIX_ROWS
    c0, c1 = PROJ_PIECES[i]
    res = _dot(hn_ref[...], win_ref[:, c0:c1])
    if c1 <= SSD_W:
        pout[:, c0:c1] = res
    elif c1 <= SSD_W + CONV_DIM:
        xcout[CONV_PAD:CONV_PAD + t, c0 - SSD_W:c1 - SSD_W] = res
    else:
        pout[:, c0 - CONV_DIM:c1 - CONV_DIM] = res


def _prep_stages(first, lb, a_neg, pin, xcin, refs):
    (cw_ref, cb_ref, dtb_ref, dsk_ref, snw_ref, hnw_ref, wout_ref, l64_ref, l16_ref, slm_ref, causal_ref,
     code_ref, bd4_ref, gmask_ref, tail_ref, act_ref, dt_ref, acum_ref, seg_ref, q_ref, k_ref, b_ref,
     sst_ref, st_ref) = refs
    t = MIX_ROWS

    xcin[0:CONV_PAD, :] = jnp.where(first, 0.0, tail_ref[...])
    tap0 = CONV_PAD - (CONV_K - 1)
    acc = cb_ref[...] + cw_ref[0:1, :] * xcin[tap0:tap0 + t, :]
    for kk in range(1, CONV_K):
        acc = acc + cw_ref[kk:kk + 1, :] * xcin[tap0 + kk:tap0 + kk + t, :]
    act_ref[...] = _silu(acc)
    tail_ref[...] = xcin[t:t + CONV_PAD, :]
    yield

    draw = pin[:, C_DT:C_DT + SSD_W] + dtb_ref[...]
    dtv = jnp.maximum(draw, 0.0) + jnp.log(1.0 + jnp.exp(-jnp.abs(draw)))
    dt_ref[...] = dtv
    d_a = dtv * a_neg
    acum_ref[...] = _dot_exact_lhs(l64_ref[...], d_a)
    seg_ref[...] = _dot_exact_lhs(l64_ref[...], d_a * slm_ref[...])
    yield

    fraw = pin[:, C_F:C_F + HG_W]
    logf = jnp.log2(lb + (1.0 - lb) * jax.nn.sigmoid(fraw))
    k_ref[...] = (1.0 - lb) * jax.nn.sigmoid(-fraw)
    q_ref[...] = _silu(pin[:, C_Q:C_Q + HG_W])
    b_ref[...] = _dot_exact_lhs(l16_ref[...], logf)
    yield


def _window_stages(first, pin, y_ref, refs):
    (cw_ref, cb_ref, dtb_ref, dsk_ref, snw_ref, hnw_ref, wout_ref, l64_ref, l16_ref, slm_ref, causal_ref,
     code_ref, bd4_ref, gmask_ref, tail_ref, act_ref, dt_ref, acum_ref, seg_ref, q_ref, k_ref, b_ref,
     sst_ref, st_ref) = refs
    nwin = MIX_ROWS // WIN
    nch = WIN // HG_CHUNK
    bd4 = bd4_ref[...]
    for w in range(nwin):
        r0 = w * WIN
        rows = slice(r0, r0 + WIN)
        restart = first if w == 0 else None

        xs = act_ref[rows, 0:SSD_W]
        bm = act_ref[rows, SSD_W:SSD_W + SSD_GN].astype(BF16)
        cm = act_ref[rows, SSD_W + SSD_GN:CONV_DIM].astype(BF16)
        ac = acum_ref[rows, :]
        a_last = acum_ref[r0 + WIN - 1:r0 + WIN, :]
        xdt = xs * dt_ref[rows, :]
        xdd = (xdt * jnp.exp2(a_last - ac)).astype(BF16)
        xdt = xdt.astype(BF16)
        lane_gn = lax.broadcasted_iota(jnp.int32, (WIN, SSD_GN), 1)
        bm0 = jnp.where(lane_gn < HEAD, bm, 0.0)
        bm1 = jnp.where(lane_gn >= HEAD, bm, 0.0)
        bm_bd = jnp.concatenate([bm0, bm1], axis=0)
        cb = lax.dot_general(cm, bm_bd, NT_DIMS, preferred_element_type=F32)
        cb_sw = pltpu.roll(cb, shift=HEAD, axis=1)
        cb_g0 = jnp.where(lane_gn < HEAD, cb, cb_sw)
        cb_g1 = jnp.where(lane_gn < HEAD, cb_sw, cb)
        cbw = jnp.concatenate([cb_g0] * 2 + [cb_g1] * 2, axis=1)
        yield
        mw = (cbw * jnp.exp2(seg_ref[rows, :]) * causal_ref[...]).astype(BF16)
        y_parts = []
        for hg in range(2):
            cs = slice(hg * PACK, (hg + 1) * PACK)
            x_bd = jnp.concatenate([xdt[:, cs]] * 4, axis=0) * bd4
            y_parts.append(_dot(mw[:, cs], x_bd))
        y = jnp.concatenate(y_parts, axis=1)
        s_prev = sst_ref[...]
        if restart is not None:
            s_prev = jnp.where(restart, 0.0, s_prev)
        y = y + _dot(cm, (s_prev * gmask_ref[...]).astype(BF16)) * jnp.exp2(ac)
        contrib = lax.dot_general(bm, xdd, TN_DIMS, preferred_element_type=F32)
        sst_ref[...] = s_prev * jnp.exp2(a_last) + contrib
        yield
        y = y + dsk_ref[...] * xs
        y = y * _silu(pin[rows, C_Z:C_Z + SSD_W])
        y2 = y * y
        ms0 = jnp.mean(y2[:, 0:PACK], axis=-1, keepdims=True)
        ms1 = jnp.mean(y2[:, PACK:2 * PACK], axis=-1, keepdims=True)
        inv = jnp.concatenate([jnp.broadcast_to(lax.rsqrt(ms0 + NORM_EPS), (WIN, PACK)),
                               jnp.broadcast_to(lax.rsqrt(ms1 + NORM_EPS), (WIN, PACK))], axis=1)
        y_ref[rows, 0:SSD_W] = (y * inv * snw_ref[...]).astype(BF16)
        yield

        code = code_ref[...]
        for hg in range(2):
            cs = slice(hg * PACK, (hg + 1) * PACK)
            q = q_ref[rows, cs]
            k = k_ref[rows, cs]
            b = b_ref[rows, cs]
            v = pin[rows, C_I + hg * PACK:C_I + (hg + 1) * PACK].astype(BF16)
            last = [r0 + HG_CHUNK * c + HG_CHUNK - 1 for c in range(nch)]
            mid = [r0 + HG_CHUNK * c + HG_CHUNK // 2 - 1 for c in range(nch)]
            bl = [b_ref[i:i + 1, cs] for i in last]
            br = [b_ref[i:i + 1, cs] for i in mid]
            zero = jnp.zeros_like(bl[0])
            gpre = [zero, bl[0], bl[0] + bl[1], bl[0] + bl[1] + bl[2]]
            gtot = gpre[3] + bl[3]
            cr = [bl[c] - br[c] for c in range(nch)]
            e_d = [None,
                   [cr[0], cr[1], cr[2]],
                   [bl[1] + cr[0], bl[2] + cr[1]],
                   [bl[2] + bl[1] + cr[0]]]

            def bc(rows_list):
                return jnp.concatenate([jnp.broadcast_to(r, (HG_CHUNK, PACK)) for r in rows_list], axis=0)

            br_bc, g_bc = bc(br), bc(gpre)
            qe = q * jnp.exp2(b - br_bc)
            ke = (k * jnp.exp2(br_bc - b)).astype(BF16)
            qb = q * jnp.exp2(b)
            q_d = [qe] + [qb[d * HG_CHUNK:, :] * jnp.exp2(bc(e_d[d])) for d in range(1, nch)]
            qg = (qb * jnp.exp2(g_bc)).astype(BF16)
            k64 = (k * jnp.exp2(gtot - g_bc - b)).astype(BF16)
            ke_bd = jnp.concatenate([ke] * 4, axis=0) * bd4
            v_bd = jnp.concatenate([v] * 4, axis=0) * bd4
            qs = jnp.concatenate(q_d, axis=0).astype(BF16)
            s_all = lax.dot_general(qs, ke_bd, NT_DIMS, preferred_element_type=F32)
            yield
            start_d = [0, WIN, 2 * WIN - HG_CHUNK, 3 * WIN - 3 * HG_CHUNK]
            sc_chunks = []
            for c in range(nch):
                crow = slice(c * HG_CHUNK, (c + 1) * HG_CHUNK)
                sc_c = jnp.zeros((HG_CHUNK, PACK), F32)
                for d in range(c, -1, -1):
                    lo = start_d[d] + (c - d) * HG_CHUNK
                    sc_c = jnp.where(code[crow, :] == d, s_all[lo:lo + HG_CHUNK, :], sc_c)
                sc_chunks.append(sc_c)
            sc = jnp.concatenate(sc_chunks, axis=0)
            o = _dot(sc.astype(BF16), v_bd)
            st = st_ref[hg]
            if restart is not None:
                st = jnp.where(restart, 0.0, st)
            o = o + lax.dot_general(qg, st.astype(BF16) * bd4, NT_DIMS, preferred_element_type=F32)
            contrib = lax.dot_general(v, k64, TN_DIMS, preferred_element_type=F32)
            st_ref[hg] = st * jnp.exp2(gtot) + contrib
            oo = o * o
            head_of_lane = lax.broadcasted_iota(jnp.int32, (WIN, PACK), 1) // HEAD
            inv = jnp.zeros((WIN, PACK), F32)
            for h in range(PACK // HEAD):
                in_head = head_of_lane == h
                ms = jnp.sum(jnp.where(in_head, oo, 0.0), axis=-1, keepdims=True) * (1.0 / HEAD)
                inv = jnp.where(in_head, lax.rsqrt(ms + NORM_EPS), inv)
            gate = _silu(pin[rows, C_G + hg * PACK:C_G + (hg + 1) * PACK])
            yh = o * inv * hnw_ref[:, cs] * gate
            y_ref[rows, SSD_W + hg * PACK:SSD_W + (hg + 1) * PACK] = yh.astype(BF16)
            yield


def _mixer_body(xb_ref, xa0_ref, xa1_ref, nw_ref, win_ref, cw_ref, cb_ref, dtb_ref, alog_ref, dsk_ref,
                snw_ref, lbl_ref, hnw_ref, wout_ref, l64_ref, l16_ref, slm_ref, causal_ref, code_ref,
                bd4_ref, gmask_ref,
                o_ref,
                p0_ref, p1_ref, xc0_ref, xc1_ref, hn_ref, tail_ref, act_ref, dt_ref, acum_ref, seg_ref,
                q_ref, k_ref, b_ref, y0_ref, y1_ref, sst_ref, st_ref, *, layer, blocks_per_seq):
    t = MIX_ROWS
    u = pl.program_id(0)

    refs = (cw_ref, cb_ref, dtb_ref, dsk_ref, snw_ref, hnw_ref, wout_ref, l64_ref, l16_ref, slm_ref, causal_ref,
            code_ref, bd4_ref, gmask_ref, tail_ref, act_ref, dt_ref, acum_ref, seg_ref, q_ref, k_ref, b_ref,
            sst_ref, st_ref)

    def layer_consts():
        p = jax.nn.softmax(lbl_ref[...], axis=0)
        lb = jnp.zeros((1, HG_W), F32)
        for j in range(1, layer + 1):
            lb = lb + p[j:j + 1, :]
        return lb, -jnp.exp(alog_ref[...]) * LOG2_E

    @pl.when(u == 0)
    def _():
        tail_ref[...] = jnp.zeros_like(tail_ref)
        sst_ref[...] = jnp.zeros_like(sst_ref)
        st_ref[...] = jnp.zeros_like(st_ref)
        hn_ref[...] = _rms(xb_ref[0:t, :], nw_ref[...]).astype(BF16)
        for i in range(N_PROJ_TILES):
            _project_tile(i, hn_ref, win_ref, p0_ref, xc0_ref)
        for _ in _prep_stages(True, *layer_consts(), p0_ref, xc0_ref, refs):
            pass
        hn_ref[...] = _rms(xb_ref[t:2 * t, :], nw_ref[...]).astype(BF16)
        for i in range(N_EARLY_TILES):
            _project_tile(i, hn_ref, win_ref, p1_ref, xc1_ref)

    lb, a_neg = layer_consts()
    for half in range(2):
        pin, xcin = (p0_ref, xc0_ref) if half == 0 else (p1_ref, xc1_ref)
        pout, xcout = (p1_ref, xc1_ref) if half == 0 else (p0_ref, xc0_ref)
        xa_ref = xa0_ref if half == 0 else xa1_ref
        y_ref = y0_ref if half == 0 else y1_ref
        first = lax.rem(2 * u + half, blocks_per_seq) == 0
        first_next = lax.rem(2 * u + half + 1, blocks_per_seq) == 0
        brow = slice(half * t, (half + 1) * t)

        tile = N_EARLY_TILES
        for n_stage, _ in enumerate(_window_stages(first, pin, y_ref, refs)):
            if n_stage % 2 == 0 and tile < N_PROJ_TILES:
                _project_tile(tile, hn_ref, win_ref, pout, xcout)
                tile += 1
        assert tile == N_PROJ_TILES
        o_ref[brow, :] = xb_ref[brow, :] + _dot(y_ref[...], wout_ref[...])
        hn_ref[...] = _rms(xa_ref[...], nw_ref[...]).astype(BF16)
        per_stage = N_EARLY_TILES // 3
        for i, _ in enumerate(_prep_stages(first_next, lb, a_neg, pout, xcout, refs)):
            for j in range(i * per_stage, (i + 1) * per_stage):
                _project_tile(j, hn_ref, win_ref, pin, xcin)


def _mixer_consts():
    t = MIX_ROWS
    r = np.arange(t)
    same64 = (r[:, None] // WIN) == (r[None, :] // WIN)
    same16 = (r[:, None] // HG_CHUNK) == (r[None, :] // HG_CHUNK)
    lower = r[None, :] <= r[:, None]
    l64 = (same64 & lower).astype(np.float32)
    l16 = (same16 & lower).astype(np.float32)
    l64 = np.concatenate([l64, l64], axis=1)
    l16 = np.concatenate([l16, l16], axis=1)
    lane = np.arange(SSD_W)
    slm = ((r[:, None] % WIN) > (lane[None, :] % HEAD)).astype(np.float32)
    tt = np.arange(WIN)
    causal = (tt[:, None] >= (lane[None, :] % HEAD)).astype(np.float32)
    ls = np.arange(PACK) % HEAD
    dch = tt[:, None] // HG_CHUNK - ls[None, :] // HG_CHUNK
    code = np.where(dch == 0, np.where(ls[None, :] <= tt[:, None], 0, 9), np.where((dch >= 1) & (dch <= 3), dch, 9))
    rp = np.arange(PACK)
    bd4 = ((rp[:, None] // HEAD) == (rp[None, :] // HEAD)).astype(np.float32)
    gn = np.arange(SSD_GN)
    gmask = ((gn[:, None] // HEAD) == (lane[None, :] // PACK)).astype(np.float32)
    return (jnp.asarray(l64, BF16), jnp.asarray(l16, BF16), jnp.asarray(slm, F32), jnp.asarray(causal, F32),
            jnp.asarray(code.astype(np.float32), F32), jnp.asarray(bd4, BF16), jnp.asarray(gmask, F32))


def _mixer(x2d, nw, win, cw, cb, dtb, alog, dsk, snw, lbl, hnw, wout, consts, layer, seq):
    n = x2d.shape[0]
    t = MIX_ROWS
    nblk = n // t
    const2 = lambda u: (0, 0)
    lsel = lambda u: (layer, 0, 0)
    full = lambda a: pl.BlockSpec(a.shape, const2)
    l64, l16, slm, causal, code, bd4, gmask = consts
    return pl.pallas_call(
        functools.partial(_mixer_body, layer=layer, blocks_per_seq=seq // t),
        grid=(nblk // 2,),
        in_specs=[
            pl.BlockSpec((2 * t, D_MODEL), lambda u: (u, 0)),
            pl.BlockSpec((t, D_MODEL), lambda u: (jnp.minimum(2 * u + 2, nblk - 1), 0)),
            pl.BlockSpec((t, D_MODEL), lambda u: (jnp.minimum(2 * u + 3, nblk - 1), 0)),
            pl.BlockSpec((None, 1, D_MODEL), lsel),
            pl.BlockSpec((None, D_MODEL, D_PROJ), lsel),
            pl.BlockSpec((None, CONV_K, CONV_DIM), lsel),
            pl.BlockSpec((None, 1, CONV_DIM), lsel),
            pl.BlockSpec((None, 1, SSD_W), lsel),
            pl.BlockSpec((None, 1, SSD_W), lsel),
            pl.BlockSpec((None, 1, SSD_W), lsel),
            pl.BlockSpec((None, 1, SSD_W), lsel),
            full(lbl),
            pl.BlockSpec((None, 1, HG_W), lsel),
            pl.BlockSpec((None, D_MODEL, D_MODEL), lsel),
            full(l64), full(l16), full(slm), full(causal), full(code), full(bd4), full(gmask),
        ],
        out_specs=pl.BlockSpec((2 * t, D_MODEL), lambda u: (u, 0)),
        out_shape=jax.ShapeDtypeStruct(x2d.shape, F32),
        scratch_shapes=[
            pltpu.VMEM((t, PROJ_W), F32),
            pltpu.VMEM((t, PROJ_W), F32),
            pltpu.VMEM((t + CONV_PAD, CONV_DIM), F32),
            pltpu.VMEM((t + CONV_PAD, CONV_DIM), F32),
            pltpu.VMEM((t, D_MODEL), BF16),
            pltpu.VMEM((CONV_PAD, CONV_DIM), F32),
            pltpu.VMEM((t, CONV_DIM), F32),
            pltpu.VMEM((t, SSD_W), F32),
            pltpu.VMEM((t, SSD_W), F32),
            pltpu.VMEM((t, SSD_W), F32),
            pltpu.VMEM((t, HG_W), F32),
            pltpu.VMEM((t, HG_W), F32),
            pltpu.VMEM((t, HG_W), F32),
            pltpu.VMEM((t, D_MODEL), BF16),
            pltpu.VMEM((t, D_MODEL), BF16),
            pltpu.VMEM((SSD_GN, SSD_W), F32),
            pltpu.VMEM((2, PACK, PACK), F32),
        ],
        compiler_params=pltpu.CompilerParams(
            dimension_semantics=("arbitrary",), vmem_limit_bytes=VMEM_LIMIT),
        name="token_mixer",
    )(x2d, x2d, x2d, nw, win, cw, cb, dtb, alog, dsk, snw, lbl, hnw, wout, l64, l16, slm, causal, code, bd4, gmask)


def kernel(x, ffn1_norm, ffn1_w_gate, ffn1_w_up, ffn1_w_down, mix_norm, w_in, ssd_conv_w, ssd_conv_b,
           ssd_dt_bias, ssd_a_log, ssd_d, ssd_norm_w, hg_lb_logits, hg_norm_w, w_out, ffn2_norm,
           ffn2_w_gate, ffn2_w_up, ffn2_w_down, final_norm):
    bsz, seq, d_model = x.shape
    depth = w_in.shape[0]
    assert d_model == D_MODEL and w_in.shape[1:] == (D_MODEL, D_PROJ_IN) and ffn1_w_gate.shape[1:] == (D_MODEL, D_FF)
    assert seq % MIX_ROWS == 0 and (bsz * seq) % (2 * MIX_ROWS) == 0 and (bsz * seq) % FFN_ROWS == 0
    rep = lambda a: jnp.repeat(a.astype(F32), HEAD, axis=-1)[:, None, :]
    row = lambda a: a.astype(F32)[:, None, :]
    w_dt = jnp.repeat(w_in[:, :, W_IN_DT0:W_IN_DT1], HEAD, axis=-1)
    win_k = jnp.concatenate([w_in[:, :, 0:W_IN_DT0], w_in[:, :, W_IN_DT1:D_PROJ_IN], w_dt], axis=-1).astype(BF16)
    wout_k = w_out.astype(BF16)
    f1 = (row(ffn1_norm), ffn1_w_gate.astype(BF16), ffn1_w_up.astype(BF16), ffn1_w_down.astype(BF16))
    f2 = (row(ffn2_norm), ffn2_w_gate.astype(BF16), ffn2_w_up.astype(BF16), ffn2_w_down.astype(BF16))
    fnw = final_norm.astype(F32)[None, :]
    hnw = jnp.tile(hg_norm_w.astype(F32), (1, HG_W // HEAD))[:, None, :]
    consts = _mixer_consts()

    xf = x.astype(F32).reshape(bsz * seq, D_MODEL)
    for l in range(depth):
        xf = _ffn(xf, *f1, fnw, l, False)
        xf = _mixer(xf, row(mix_norm), win_k, ssd_conv_w.astype(F32), row(ssd_conv_b), rep(ssd_dt_bias),
                    rep(ssd_a_log), rep(ssd_d), row(ssd_norm_w), hg_lb_logits.astype(F32), hnw, wout_k,
                    consts, l, seq)
        xf = _ffn(xf, *f2, fnw, l, l == depth - 1)
    return xf.reshape(bsz, seq, D_MODEL).astype(x.dtype)
```

```python
import functools

import numpy as np
import jax
import jax.numpy as jnp
from jax import lax
from jax.experimental import pallas as pl
from jax.experimental.pallas import tpu as pltpu

F32 = jnp.float32
BF16 = jnp.bfloat16

D_MODEL = 1024
D_FF = 2816
SSD_W = 512
SSD_GN = 128
CONV_DIM = SSD_W + 2 * SSD_GN
CONV_K = 4
CONV_PAD = 8
HG_W = 512
HEAD = 64
WIN = 64
HG_CHUNK = 16
PACK = 256
NORM_EPS = 1e-6
LOG2_E = 1.4426950408889634

D_PROJ_IN = 3336
W_IN_DT0 = SSD_W + CONV_DIM
W_IN_DT1 = W_IN_DT0 + SSD_W // HEAD
D_PROJ = 3840

FFN_ROWS = 1024
FFN_SUB = 256
FFN_COLS = 256
MIX_ROWS = 256
V7X_VMEM_BYTES = 64 * 1024 * 1024
VMEM_LIMIT = V7X_VMEM_BYTES * 7 // 8

NT_DIMS = (((1,), (1,)), ((), ()))
TN_DIMS = (((0,), (0,)), ((), ()))


def _rms(x, w):
    return x * lax.rsqrt(jnp.mean(x * x, axis=-1, keepdims=True) + NORM_EPS) * w


def _silu(x):
    return x * jax.nn.sigmoid(x)


def _dot(a, b):
    return jnp.dot(a, b, preferred_element_type=F32)


def _dot_exact_lhs(mm, x):
    hi = x.astype(BF16)
    lo = (x - hi.astype(F32)).astype(BF16)
    return _dot(mm, jnp.concatenate([hi, lo], axis=0))


def _ffn_body(x_ref, nw_ref, wg_ref, wu_ref, wd_ref, fnw_ref, o_ref, h_ref, *, final):
    for blk in range(FFN_ROWS // FFN_SUB):
        rows = slice(blk * FFN_SUB, (blk + 1) * FFN_SUB)
        x = x_ref[rows, :]
        xn = _rms(x, nw_ref[...]).astype(BF16)
        for c in range(D_FF // FFN_COLS):
            cols = slice(c * FFN_COLS, (c + 1) * FFN_COLS)
            g = _dot(xn, wg_ref[:, cols])
            u = _dot(xn, wu_ref[:, cols])
            h_ref[:, cols] = (_silu(g) * u).astype(BF16)
        y = x + 0.5 * _dot(h_ref[...], wd_ref[...])
        if final:
            y = _rms(y, fnw_ref[...])
        o_ref[rows, :] = y


def _ffn(x2d, nw, wg, wu, wd, fnw, layer, final):
    n = x2d.shape[0]
    const = lambda i: (0, 0)
    wsel = lambda i: (layer, 0, 0)
    return pl.pallas_call(
        functools.partial(_ffn_body, final=final),
        grid=(n // FFN_ROWS,),
        in_specs=[
            pl.BlockSpec((FFN_ROWS, D_MODEL), lambda i: (i, 0)),
            pl.BlockSpec((None, 1, D_MODEL), wsel),
            pl.BlockSpec((None, D_MODEL, D_FF), wsel),
            pl.BlockSpec((None, D_MODEL, D_FF), wsel),
            pl.BlockSpec((None, D_FF, D_MODEL), wsel),
            pl.BlockSpec((1, D_MODEL), const),
        ],
        out_specs=pl.BlockSpec((FFN_ROWS, D_MODEL), lambda i: (i, 0)),
        out_shape=jax.ShapeDtypeStruct((n, D_MODEL), F32),
        scratch_shapes=[pltpu.VMEM((FFN_SUB, D_FF), BF16)],
        compiler_params=pltpu.CompilerParams(
            dimension_semantics=("arbitrary",), vmem_limit_bytes=VMEM_LIMIT),
        name="ffn_half_step",
    )(x2d, nw, wg, wu, wd, fnw)


C_Z, C_Q, C_F, C_I, C_G, C_DT = 0, 512, 1024, 1536, 2048, 2560
PROJ_W = 3072
PROJ_PIECES = tuple((c, c + 256) for c in range(0, D_PROJ, 256))
N_PROJ_TILES = len(PROJ_PIECES)
N_EARLY_TILES = 3


def _project_tile(i, hn_ref, win_ref, pout, xcout):
    t = MIX_ROWS
    c0, c1 = PROJ_PIECES[i]
    res = _dot(hn_ref[...], win_ref[:, c0:c1])
    if c1 <= SSD_W:
        pout[:, c0:c1] = res
    elif c1 <= SSD_W + CONV_DIM:
        xcout[CONV_PAD:CONV_PAD + t, c0 - SSD_W:c1 - SSD_W] = res
    else:
        pout[:, c0 - CONV_DIM:c1 - CONV_DIM] = res


def _prep_stages(first, lb, a_neg, pin, xcin, refs):
    (cw_ref, cb_ref, dtb_ref, dsk_ref, snw_ref, hnw_ref, wout_ref, l64_ref, l16_ref, slm_ref, causal_ref,
     code_ref, bd4_ref, gmask_ref, tail_ref, act_ref, dt_ref, acum_ref, seg_ref, q_ref, k_ref, b_ref,
     sst_ref, st_ref) = refs
    t = MIX_ROWS

    xcin[0:CONV_PAD, :] = jnp.where(first, 0.0, tail_ref[...])
    tap0 = CONV_PAD - (CONV_K - 1)
    acc = cb_ref[...] + cw_ref[0:1, :] * xcin[tap0:tap0 + t, :]
    for kk in range(1, CONV_K):
        acc = acc + cw_ref[kk:kk + 1, :] * xcin[tap0 + kk:tap0 + kk + t, :]
    act_ref[...] = _silu(acc)
    tail_ref[...] = xcin[t:t + CONV_PAD, :]
    yield

    draw = pin[:, C_DT:C_DT + SSD_W] + dtb_ref[...]
    dtv = jnp.maximum(draw, 0.0) + jnp.log(1.0 + jnp.exp(-jnp.abs(draw)))
    dt_ref[...] = dtv
    d_a = dtv * a_neg
    acum_ref[...] = _dot_exact_lhs(l64_ref[...], d_a)
    seg_ref[...] = _dot_exact_lhs(l64_ref[...], d_a * slm_ref[...])
    yield

    fraw = pin[:, C_F:C_F + HG_W]
    logf = jnp.log2(lb + (1.0 - lb) * jax.nn.sigmoid(fraw))
    k_ref[...] = (1.0 - lb) * jax.nn.sigmoid(-fraw)
    q_ref[...] = _silu(pin[:, C_Q:C_Q + HG_W])
    b_ref[...] = _dot_exact_lhs(l16_ref[...], logf)
    yield


def _window_stages(first, pin, y_ref, refs):
    (cw_ref, cb_ref, dtb_ref, dsk_ref, snw_ref, hnw_ref, wout_ref, l64_ref, l16_ref, slm_ref, causal_ref,
     code_ref, bd4_ref, gmask_ref, tail_ref, act_ref, dt_ref, acum_ref, seg_ref, q_ref, k_ref, b_ref,
     sst_ref, st_ref) = refs
    nwin = MIX_ROWS // WIN
    nch = WIN // HG_CHUNK
    bd4 = bd4_ref[...]
    for w in range(nwin):
        r0 = w * WIN
        rows = slice(r0, r0 + WIN)
        restart = first if w == 0 else None

        xs = act_ref[rows, 0:SSD_W]
        bm = act_ref[rows, SSD_W:SSD_W + SSD_GN].astype(BF16)
        cm = act_ref[rows, SSD_W + SSD_GN:CONV_DIM].astype(BF16)
        ac = acum_ref[rows, :]
        a_last = acum_ref[r0 + WIN - 1:r0 + WIN, :]
        xdt = xs * dt_ref[rows, :]
        xdd = (xdt * jnp.exp2(a_last - ac)).astype(BF16)
        xdt = xdt.astype(BF16)
        lane_gn = lax.broadcasted_iota(jnp.int32, (WIN, SSD_GN), 1)
        bm0 = jnp.where(lane_gn < HEAD, bm, 0.0)
        bm1 = jnp.where(lane_gn >= HEAD, bm, 0.0)
        bm_bd = jnp.concatenate([bm0, bm1], axis=0)
        cb = lax.dot_general(cm, bm_bd, NT_DIMS, preferred_element_type=F32)
        cb_sw = pltpu.roll(cb, shift=HEAD, axis=1)
        cb_g0 = jnp.where(lane_gn < HEAD, cb, cb_sw)
        cb_g1 = jnp.where(lane_gn < HEAD, cb_sw, cb)
        cbw = jnp.concatenate([cb_g0] * 2 + [cb_g1] * 2, axis=1)
        yield
        mw = (cbw * jnp.exp2(seg_ref[rows, :]) * causal_ref[...]).astype(BF16)
        y_parts = []
        for hg in range(2):
            cs = slice(hg * PACK, (hg + 1) * PACK)
            x_bd = jnp.concatenate([xdt[:, cs]] * 4, axis=0) * bd4
            y_parts.append(_dot(mw[:, cs], x_bd))
        y = jnp.concatenate(y_parts, axis=1)
        s_prev = sst_ref[...]
        if restart is not None:
            s_prev = jnp.where(restart, 0.0, s_prev)
        y = y + _dot(cm, (s_prev * gmask_ref[...]).astype(BF16)) * jnp.exp2(ac)
        contrib = lax.dot_general(bm, xdd, TN_DIMS, preferred_element_type=F32)
        sst_ref[...] = s_prev * jnp.exp2(a_last) + contrib
        yield
        y = y + dsk_ref[...] * xs
        y = y * _silu(pin[rows, C_Z:C_Z + SSD_W])
        y2 = y * y
        ms0 = jnp.mean(y2[:, 0:PACK], axis=-1, keepdims=True)
        ms1 = jnp.mean(y2[:, PACK:2 * PACK], axis=-1, keepdims=True)
        inv = jnp.concatenate([jnp.broadcast_to(lax.rsqrt(ms0 + NORM_EPS), (WIN, PACK)),
                               jnp.broadcast_to(lax.rsqrt(ms1 + NORM_EPS), (WIN, PACK))], axis=1)
        y_ref[rows, 0:SSD_W] = (y * inv * snw_ref[...]).astype(BF16)
        yield

        code = code_ref[...]
        for hg in range(2):
            cs = slice(hg * PACK, (hg + 1) * PACK)
            q = q_ref[rows, cs]
            k = k_ref[rows, cs]
            b = b_ref[rows, cs]
            v = pin[rows, C_I + hg * PACK:C_I + (hg + 1) * PACK].astype(BF16)
            last = [r0 + HG_CHUNK * c + HG_CHUNK - 1 for c in range(nch)]
            mid = [r0 + HG_CHUNK * c + HG_CHUNK // 2 - 1 for c in range(nch)]
            bl = [b_ref[i:i + 1, cs] for i in last]
            br = [b_ref[i:i + 1, cs] for i in mid]
            zero = jnp.zeros_like(bl[0])
            gpre = [zero, bl[0], bl[0] + bl[1], bl[0] + bl[1] + bl[2]]
            gtot = gpre[3] + bl[3]
            cr = [bl[c] - br[c] for c in range(nch)]
            e_d = [None,
                   [cr[0], cr[1], cr[2]],
                   [bl[1] + cr[0], bl[2] + cr[1]],
                   [bl[2] + bl[1] + cr[0]]]

            def bc(rows_list):
                return jnp.concatenate([jnp.broadcast_to(r, (HG_CHUNK, PACK)) for r in rows_list], axis=0)

            br_bc, g_bc = bc(br), bc(gpre)
            qe = q * jnp.exp2(b - br_bc)
            ke = (k * jnp.exp2(br_bc - b)).astype(BF16)
            qb = q * jnp.exp2(b)
            q_d = [qe] + [qb[d * HG_CHUNK:, :] * jnp.exp2(bc(e_d[d])) for d in range(1, nch)]
            qg = (qb * jnp.exp2(g_bc)).astype(BF16)
            k64 = (k * jnp.exp2(gtot - g_bc - b)).astype(BF16)
            ke_bd = jnp.concatenate([ke] * 4, axis=0) * bd4
            v_bd = jnp.concatenate([v] * 4, axis=0) * bd4
            qs = jnp.concatenate(q_d, axis=0).astype(BF16)
            s_all = lax.dot_general(qs, ke_bd, NT_DIMS, preferred_element_type=F32)
            yield
            start_d = [0, WIN, 2 * WIN - HG_CHUNK, 3 * WIN - 3 * HG_CHUNK]
            sc_chunks = []
            for c in range(nch):
                crow = slice(c * HG_CHUNK, (c + 1) * HG_CHUNK)
                sc_c = jnp.zeros((HG_CHUNK, PACK), F32)
                for d in range(c, -1, -1):
                    lo = start_d[d] + (c - d) * HG_CHUNK
                    sc_c = jnp.where(code[crow, :] == d, s_all[lo:lo + HG_CHUNK, :], sc_c)
                sc_chunks.append(sc_c)
            sc = jnp.concatenate(sc_chunks, axis=0)
            o = _dot(sc.astype(BF16), v_bd)
            st = st_ref[hg]
            if restart is not None:
                st = jnp.where(restart, 0.0, st)
            o = o + lax.dot_general(qg, st.astype(BF16) * bd4, NT_DIMS, preferred_element_type=F32)
            contrib = lax.dot_general(v, k64, TN_DIMS, preferred_element_type=F32)
            st_ref[hg] = st * jnp.exp2(gtot) + contrib
            oo = o * o
            head_of_lane = lax.broadcasted_iota(jnp.int32, (WIN, PACK), 1) // HEAD
            inv = jnp.zeros((WIN, PACK), F32)
            for h in range(PACK // HEAD):
                in_head = head_of_lane == h
                ms = jnp.sum(jnp.where(in_head, oo, 0.0), axis=-1, keepdims=True) * (1.0 / HEAD)
                inv = jnp.where(in_head, lax.rsqrt(ms + NORM_EPS), inv)
            gate = _silu(pin[rows, C_G + hg * PACK:C_G + (hg + 1) * PACK])
            yh = o * inv * hnw_ref[:, cs] * gate
            y_ref[rows, SSD_W + hg * PACK:SSD_W + (hg + 1) * PACK] = yh.astype(BF16)
            yield


def _mixer_body(xb_ref, xa0_ref, xa1_ref, nw_ref, win_ref, cw_ref, cb_ref, dtb_ref, alog_ref, dsk_ref,
                snw_ref, lbl_ref, hnw_ref, wout_ref, l64_ref, l16_ref, slm_ref, causal_ref, code_ref,
                bd4_ref, gmask_ref,
                o_ref,
                p0_ref, p1_ref, xc0_ref, xc1_ref, hn_ref, tail_ref, act_ref, dt_ref, acum_ref, seg_ref,
                q_ref, k_ref, b_ref, y0_ref, y1_ref, sst_ref, st_ref, *, layer, blocks_per_seq):
    t = MIX_ROWS
    u = pl.program_id(0)

    refs = (cw_ref, cb_ref, dtb_ref, dsk_ref, snw_ref, hnw_ref, wout_ref, l64_ref, l16_ref, slm_ref, causal_ref,
            code_ref, bd4_ref, gmask_ref, tail_ref, act_ref, dt_ref, acum_ref, seg_ref, q_ref, k_ref, b_ref,
            sst_ref, st_ref)

    def layer_consts():
        p = jax.nn.softmax(lbl_ref[...], axis=0)
        lb = jnp.zeros((1, HG_W), F32)
        for j in range(1, layer + 1):
            lb = lb + p[j:j + 1, :]
        return lb, -jnp.exp(alog_ref[...]) * LOG2_E

    @pl.when(u == 0)
    def _():
        tail_ref[...] = jnp.zeros_like(tail_ref)
        sst_ref[...] = jnp.zeros_like(sst_ref)
        st_ref[...] = jnp.zeros_like(st_ref)
        hn_ref[...] = _rms(xb_ref[0:t, :], nw_ref[...]).astype(BF16)
        for i in range(N_PROJ_TILES):
            _project_tile(i, hn_ref, win_ref, p0_ref, xc0_ref)
        for _ in _prep_stages(True, *layer_consts(), p0_ref, xc0_ref, refs):
            pass
        hn_ref[...] = _rms(xb_ref[t:2 * t, :], nw_ref[...]).astype(BF16)
        for i in range(N_EARLY_TILES):
            _project_tile(i, hn_ref, win_ref, p1_ref, xc1_ref)

    lb, a_neg = layer_consts()
    for half in range(2):
        pin, xcin = (p0_ref, xc0_ref) if half == 0 else (p1_ref, xc1_ref)
        pout, xcout = (p1_ref, xc1_ref) if half == 0 else (p0_ref, xc0_ref)
        xa_ref = xa0_ref if half == 0 else xa1_ref
        y_ref = y0_ref if half == 0 else y1_ref
        first = lax.rem(2 * u + half, blocks_per_seq) == 0
        first_next = lax.rem(2 * u + half + 1, blocks_per_seq) == 0
        brow = slice(half * t, (half + 1) * t)

        tile = N_EARLY_TILES
        for n_stage, _ in enumerate(_window_stages(first, pin, y_ref, refs)):
            if n_stage % 2 == 0 and tile < N_PROJ_TILES:
                _project_tile(tile, hn_ref, win_ref, pout, xcout)
                tile += 1
        assert tile == N_PROJ_TILES
        o_ref[brow, :] = xb_ref[brow, :] + _dot(y_ref[...], wout_ref[...])
        hn_ref[...] = _rms(xa_ref[...], nw_ref[...]).astype(BF16)
        per_stage = N_EARLY_TILES // 3
        for i, _ in enumerate(_prep_stages(first_next, lb, a_neg, pout, xcout, refs)):
            for j in range(i * per_stage, (i + 1) * per_stage):
                _project_tile(j, hn_ref, win_ref, pin, xcin)


def _mixer_consts():
    t = MIX_ROWS
    r = np.arange(t)
    same64 = (r[:, None] // WIN) == (r[None, :] // WIN)
    same16 = (r[:, None] // HG_CHUNK) == (r[None, :] // HG_CHUNK)
    lower = r[None, :] <= r[:, None]
    l64 = (same64 & lower).astype(np.float32)
    l16 = (same16 & lower).astype(np.float32)
    l64 = np.concatenate([l64, l64], axis=1)
    l16 = np.concatenate([l16, l16], axis=1)
    lane = np.arange(SSD_W)
    slm = ((r[:, None] % WIN) > (lane[None, :] % HEAD)).astype(np.float32)
    tt = np.arange(WIN)
    causal = (tt[:, None] >= (lane[None, :] % HEAD)).astype(np.float32)
    ls = np.arange(PACK) % HEAD
    dch = tt[:, None] // HG_CHUNK - ls[None, :] // HG_CHUNK
    code = np.where(dch == 0, np.where(ls[None, :] <= tt[:, None], 0, 9), np.where((dch >= 1) & (dch <= 3), dch, 9))
    rp = np.arange(PACK)
    bd4 = ((rp[:, None] // HEAD) == (rp[None, :] // HEAD)).astype(np.float32)
    gn = np.arange(SSD_GN)
    gmask = ((gn[:, None] // HEAD) == (lane[None, :] // PACK)).astype(np.float32)
    return (jnp.asarray(l64, BF16), jnp.asarray(l16, BF16), jnp.asarray(slm, F32), jnp.asarray(causal, F32),
            jnp.asarray(code.astype(np.float32), F32), jnp.asarray(bd4, BF16), jnp.asarray(gmask, F32))


def _mixer(x2d, nw, win, cw, cb, dtb, alog, dsk, snw, lbl, hnw, wout, consts, layer, seq):
    n = x2d.shape[0]
    t = MIX_ROWS
    nblk = n // t
    const2 = lambda u: (0, 0)
    lsel = lambda u: (layer, 0, 0)
    full = lambda a: pl.BlockSpec(a.shape, const2)
    l64, l16, slm, causal, code, bd4, gmask = consts
    return pl.pallas_call(
        functools.partial(_mixer_body, layer=layer, blocks_per_seq=seq // t),
        grid=(nblk // 2,),
        in_specs=[
            pl.BlockSpec((2 * t, D_MODEL), lambda u: (u, 0)),
            pl.BlockSpec((t, D_MODEL), lambda u: (jnp.minimum(2 * u + 2, nblk - 1), 0)),
            pl.BlockSpec((t, D_MODEL), lambda u: (jnp.minimum(2 * u + 3, nblk - 1), 0)),
            pl.BlockSpec((None, 1, D_MODEL), lsel),
            pl.BlockSpec((None, D_MODEL, D_PROJ), lsel),
            pl.BlockSpec((None, CONV_K, CONV_DIM), lsel),
            pl.BlockSpec((None, 1, CONV_DIM), lsel),
            pl.BlockSpec((None, 1, SSD_W), lsel),
            pl.BlockSpec((None, 1, SSD_W), lsel),
            pl.BlockSpec((None, 1, SSD_W), lsel),
            pl.BlockSpec((None, 1, SSD_W), lsel),
            full(lbl),
            pl.BlockSpec((None, 1, HG_W), lsel),
            pl.BlockSpec((None, D_MODEL, D_MODEL), lsel),
            full(l64), full(l16), full(slm), full(causal), full(code), full(bd4), full(gmask),
        ],
        out_specs=pl.BlockSpec((2 * t, D_MODEL), lambda u: (u, 0)),
        out_shape=jax.ShapeDtypeStruct(x2d.shape, F32),
        scratch_shapes=[
            pltpu.VMEM((t, PROJ_W), F32),
            pltpu.VMEM((t, PROJ_W), F32),
            pltpu.VMEM((t + CONV_PAD, CONV_DIM), F32),
            pltpu.VMEM((t + CONV_PAD, CONV_DIM), F32),
            pltpu.VMEM((t, D_MODEL), BF16),
            pltpu.VMEM((CONV_PAD, CONV_DIM), F32),
            pltpu.VMEM((t, CONV_DIM), F32),
            pltpu.VMEM((t, SSD_W), F32),
            pltpu.VMEM((t, SSD_W), F32),
            pltpu.VMEM((t, SSD_W), F32),
            pltpu.VMEM((t, HG_W), F32),
            pltpu.VMEM((t, HG_W), F32),
            pltpu.VMEM((t, HG_W), F32),
            pltpu.VMEM((t, D_MODEL), BF16),
            pltpu.VMEM((t, D_MODEL), BF16),
            pltpu.VMEM((SSD_GN, SSD_W), F32),
            pltpu.VMEM((2, PACK, PACK), F32),
        ],
        compiler_params=pltpu.CompilerParams(
            dimension_semantics=("arbitrary",), vmem_limit_bytes=VMEM_LIMIT),
        name="token_mixer",
    )(x2d, x2d, x2d, nw, win, cw, cb, dtb, alog, dsk, snw, lbl, hnw, wout, l64, l16, slm, causal, code, bd4, gmask)


def kernel(x, ffn1_norm, ffn1_w_gate, ffn1_w_up, ffn1_w_down, mix_norm, w_in, ssd_conv_w, ssd_conv_b,
           ssd_dt_bias, ssd_a_log, ssd_d, ssd_norm_w, hg_lb_logits, hg_norm_w, w_out, ffn2_norm,
           ffn2_w_gate, ffn2_w_up, ffn2_w_down, final_norm):
    bsz, seq, d_model = x.shape
    depth = w_in.shape[0]
    assert d_model == D_MODEL and w_in.shape[1:] == (D_MODEL, D_PROJ_IN) and ffn1_w_gate.shape[1:] == (D_MODEL, D_FF)
    assert seq % MIX_ROWS == 0 and (bsz * seq) % (2 * MIX_ROWS) == 0 and (bsz * seq) % FFN_ROWS == 0
    rep = lambda a: jnp.repeat(a.astype(F32), HEAD, axis=-1)[:, None, :]
    row = lambda a: a.astype(F32)[:, None, :]
    w_dt = jnp.repeat(w_in[:, :, W_IN_DT0:W_IN_DT1], HEAD, axis=-1)
    win_k = jnp.concatenate([w_in[:, :, 0:W_IN_DT0], w_in[:, :, W_IN_DT1:D_PROJ_IN], w_dt], axis=-1).astype(BF16)
    wout_k = w_out.astype(BF16)
    f1 = (row(ffn1_norm), ffn1_w_gate.astype(BF16), ffn1_w_up.astype(BF16), ffn1_w_down.astype(BF16))
    f2 = (row(ffn2_norm), ffn2_w_gate.astype(BF16), ffn2_w_up.astype(BF16), ffn2_w_down.astype(BF16))
    fnw = final_norm.astype(F32)[None, :]
    hnw = jnp.tile(hg_norm_w.astype(F32), (1, HG_W // HEAD))[:, None, :]
    consts = _mixer_consts()

    xf = x.astype(F32).reshape(bsz * seq, D_MODEL)
    for l in range(depth):
        xf = _ffn(xf, *f1, fnw, l, False)
        xf = _mixer(xf, row(mix_norm), win_k, ssd_conv_w.astype(F32), row(ssd_conv_b), rep(ssd_dt_bias),
                    rep(ssd_a_log), rep(ssd_d), row(ssd_norm_w), hg_lb_logits.astype(F32), hnw, wout_k,
                    consts, l, seq)
        xf = _ffn(xf, *f2, fnw, l, l == depth - 1)
    return xf.reshape(bsz, seq, D_MODEL).astype(x.dtype)
```

```python
import functools

import numpy as np
import jax
import jax.numpy as jnp
from jax import lax
from jax.experimental import pallas as pl
from jax.experimental.pallas import tpu as pltpu

F32 = jnp.float32
BF16 = jnp.bfloat16

D_MODEL = 1024
D_FF = 2816
SSD_W = 512
SSD_GN = 128
CONV_DIM = SSD_W + 2 * SSD_GN
CONV_K = 4
CONV_PAD = 8
HG_W = 512
HEAD = 64
WIN = 64
HG_CHUNK = 16
PACK = 256
NORM_EPS = 1e-6
LOG2_E = 1.4426950408889634

D_PROJ_IN = 3336
W_IN_DT0 = SSD_W + CONV_DIM
W_IN_DT1 = W_IN_DT0 + SSD_W // HEAD
D_PROJ = 3840

FFN_ROWS = 1024
FFN_SUB = 256
FFN_COLS = 256
MIX_ROWS = 256
V7X_VMEM_BYTES = 64 * 1024 * 1024
VMEM_LIMIT = V7X_VMEM_BYTES * 7 // 8

NT_DIMS = (((1,), (1,)), ((), ()))
TN_DIMS = (((0,), (0,)), ((), ()))


def _rms(x, w):
    return x * lax.rsqrt(jnp.mean(x * x, axis=-1, keepdims=True) + NORM_EPS) * w


def _silu(x):
    return x * jax.nn.sigmoid(x)


def _dot(a, b):
    return jnp.dot(a, b, preferred_element_type=F32)


def _dot_exact_lhs(mm, x):
    hi = x.astype(BF16)
    lo = (x - hi.astype(F32)).astype(BF16)
    return _dot(mm, jnp.concatenate([hi, lo], axis=0))


def _ffn_body(x_ref, nw_ref, wg_ref, wu_ref, wd_ref, fnw_ref, o_ref, h_ref, *, final):
    for blk in range(FFN_ROWS // FFN_SUB):
        rows = slice(blk * FFN_SUB, (blk + 1) * FFN_SUB)
        x = x_ref[rows, :]
        xn = _rms(x, nw_ref[...]).astype(BF16)
        for c in range(D_FF // FFN_COLS):
            cols = slice(c * FFN_COLS, (c + 1) * FFN_COLS)
            g = _dot(xn, wg_ref[:, cols])
            u = _dot(xn, wu_ref[:, cols])
            h_ref[:, cols] = (_silu(g) * u).astype(BF16)
        y = x + 0.5 * _dot(h_ref[...], wd_ref[...])
        if final:
            y = _rms(y, fnw_ref[...])
        o_ref[rows, :] = y


def _ffn(x2d, nw, wg, wu, wd, fnw, layer, final):
    n = x2d.shape[0]
    const = lambda i: (0, 0)
    wsel = lambda i: (layer, 0, 0)
    return pl.pallas_call(
        functools.partial(_ffn_body, final=final),
        grid=(n // FFN_ROWS,),
        in_specs=[
            pl.BlockSpec((FFN_ROWS, D_MODEL), lambda i: (i, 0)),
            pl.BlockSpec((None, 1, D_MODEL), wsel),
            pl.BlockSpec((None, D_MODEL, D_FF), wsel),
            pl.BlockSpec((None, D_MODEL, D_FF), wsel),
            pl.BlockSpec((None, D_FF, D_MODEL), wsel),
            pl.BlockSpec((1, D_MODEL), const),
        ],
        out_specs=pl.BlockSpec((FFN_ROWS, D_MODEL), lambda i: (i, 0)),
        out_shape=jax.ShapeDtypeStruct((n, D_MODEL), F32),
        scratch_shapes=[pltpu.VMEM((FFN_SUB, D_FF), BF16)],
        compiler_params=pltpu.CompilerParams(
            dimension_semantics=("arbitrary",), vmem_limit_bytes=VMEM_LIMIT),
        name="ffn_half_step",
    )(x2d, nw, wg, wu, wd, fnw)


C_Z, C_Q, C_F, C_I, C_G, C_DT = 0, 512, 1024, 1536, 2048, 2560
PROJ_W = 3072
PROJ_PIECES = tuple((c, c + 256) for c in range(0, D_PROJ, 256))
N_PROJ_TILES = len(PROJ_PIECES)
N_EARLY_TILES = 3


def _project_tile(i, hn_ref, win_ref, pout, xcout):
    t = MIX_ROWS
    c0, c1 = PROJ_PIECES[i]
    res = _dot(hn_ref[...], win_ref[:, c0:c1])
    if c1 <= SSD_W:
        pout[:, c0:c1] = res
    elif c1 <= SSD_W + CONV_DIM:
        xcout[CONV_PAD:CONV_PAD + t, c0 - SSD_W:c1 - SSD_W] = res
    else:
        pout[:, c0 - CONV_DIM:c1 - CONV_DIM] = res


def _prep_stages(first, lb, a_neg, pin, xcin, refs):
    (cw_ref, cb_ref, dtb_ref, dsk_ref, snw_ref, hnw_ref, wout_ref, l64_ref, l16_ref, slm_ref, causal_ref,
     code_ref, bd4_ref, tail_ref, act_ref, dt_ref, acum_ref, seg_ref, q_ref, k_ref, b_ref,
     sst_ref, st_ref) = refs
    t = MIX_ROWS

    xcin[0:CONV_PAD, :] = jnp.where(first, 0.0, tail_ref[...])
    tap0 = CONV_PAD - (CONV_K - 1)
    acc = cb_ref[...] + cw_ref[0:1, :] * xcin[tap0:tap0 + t, :]
    for kk in range(1, CONV_K):
        acc = acc + cw_ref[kk:kk + 1, :] * xcin[tap0 + kk:tap0 + kk + t, :]
    act_ref[...] = _silu(acc)
    tail_ref[...] = xcin[t:t + CONV_PAD, :]
    yield

    draw = pin[:, C_DT:C_DT + SSD_W] + dtb_ref[...]
    dtv = jnp.maximum(draw, 0.0) + jnp.log(1.0 + jnp.exp(-jnp.abs(draw)))
    dt_ref[...] = dtv
    d_a = dtv * a_neg
    acum_ref[...] = _dot_exact_lhs(l64_ref[...], d_a)
    seg_ref[...] = _dot_exact_lhs(l64_ref[...], d_a * slm_ref[...])
    yield

    fraw = pin[:, C_F:C_F + HG_W]
    logf = jnp.log2(lb + (1.0 - lb) * jax.nn.sigmoid(fraw))
    k_ref[...] = (1.0 - lb) * jax.nn.sigmoid(-fraw)
    q_ref[...] = _silu(pin[:, C_Q:C_Q + HG_W])
    b_ref[...] = _dot_exact_lhs(l16_ref[...], logf)
    yield


def _window_stages(first, pin, y_ref, refs):
    (cw_ref, cb_ref, dtb_ref, dsk_ref, snw_ref, hnw_ref, wout_ref, l64_ref, l16_ref, slm_ref, causal_ref,
     code_ref, bd4_ref, tail_ref, act_ref, dt_ref, acum_ref, seg_ref, q_ref, k_ref, b_ref,
     sst_ref, st_ref) = refs
    nwin = MIX_ROWS // WIN
    nch = WIN // HG_CHUNK
    bd4 = bd4_ref[...]
    for w in range(nwin):
        r0 = w * WIN
        rows = slice(r0, r0 + WIN)
        restart = first if w == 0 else None

        xs = act_ref[rows, 0:SSD_W]
        bm = act_ref[rows, SSD_W:SSD_W + SSD_GN].astype(BF16)
        cm = act_ref[rows, SSD_W + SSD_GN:CONV_DIM].astype(BF16)
        ac = acum_ref[rows, :]
        a_last = acum_ref[r0 + WIN - 1:r0 + WIN, :]
        xdt = xs * dt_ref[rows, :]
        xdd = (xdt * jnp.exp2(a_last - ac)).astype(BF16)
        xdt = xdt.astype(BF16)
        lane_gn = lax.broadcasted_iota(jnp.int32, (WIN, SSD_GN), 1)
        bm0 = jnp.where(lane_gn < HEAD, bm, 0.0)
        bm1 = jnp.where(lane_gn >= HEAD, bm, 0.0)
        bm_bd = jnp.concatenate([bm0, bm1], axis=0)
        cb = lax.dot_general(cm, bm_bd, NT_DIMS, preferred_element_type=F32)
        cb_sw = pltpu.roll(cb, shift=HEAD, axis=1)
        cb_g0 = jnp.where(lane_gn < HEAD, cb, cb_sw)
        cb_g1 = jnp.where(lane_gn < HEAD, cb_sw, cb)
        cbw = jnp.concatenate([cb_g0] * 2 + [cb_g1] * 2, axis=1)
        yield
        mw = (cbw * jnp.exp2(seg_ref[rows, :]) * causal_ref[...]).astype(BF16)
        y_parts = []
        for hg in range(2):
            cs = slice(hg * PACK, (hg + 1) * PACK)
            x_bd = jnp.concatenate([xdt[:, cs]] * 4, axis=0) * bd4
            y_parts.append(_dot(mw[:, cs], x_bd))
        y = jnp.concatenate(y_parts, axis=1)
        s_prev = sst_ref[...]
        if restart is not None:
            s_prev = jnp.where(restart, 0.0, s_prev)
        s_bf = s_prev.astype(BF16)
        y_inter, contrib = [], []
        for g in range(2):
            cs = slice(g * PACK, (g + 1) * PACK)
            gcol = SSD_W + g * HEAD
            bm_g = act_ref[rows, gcol:gcol + HEAD].astype(BF16)
            cm_g = act_ref[rows, gcol + SSD_GN:gcol + SSD_GN + HEAD].astype(BF16)
            y_inter.append(_dot(cm_g, s_bf[:, cs]))
            contrib.append(lax.dot_general(bm_g, xdd[:, cs], TN_DIMS, preferred_element_type=F32))
        y = y + jnp.concatenate(y_inter, axis=1) * jnp.exp2(ac)
        sst_ref[...] = s_prev * jnp.exp2(a_last) + jnp.concatenate(contrib, axis=1)
        yield
        y = y + dsk_ref[...] * xs
        y = y * _silu(pin[rows, C_Z:C_Z + SSD_W])
        y2 = y * y
        ms0 = jnp.mean(y2[:, 0:PACK], axis=-1, keepdims=True)
        ms1 = jnp.mean(y2[:, PACK:2 * PACK], axis=-1, keepdims=True)
        inv = jnp.concatenate([jnp.broadcast_to(lax.rsqrt(ms0 + NORM_EPS), (WIN, PACK)),
                               jnp.broadcast_to(lax.rsqrt(ms1 + NORM_EPS), (WIN, PACK))], axis=1)
        y_ref[rows, 0:SSD_W] = (y * inv * snw_ref[...]).astype(BF16)
        yield

        code = code_ref[...]
        for hg in range(2):
            cs = slice(hg * PACK, (hg + 1) * PACK)
            q = q_ref[rows, cs]
            k = k_ref[rows, cs]
            b = b_ref[rows, cs]
            v = pin[rows, C_I + hg * PACK:C_I + (hg + 1) * PACK].astype(BF16)
            last = [r0 + HG_CHUNK * c + HG_CHUNK - 1 for c in range(nch)]
            mid = [r0 + HG_CHUNK * c + HG_CHUNK // 2 - 1 for c in range(nch)]
            bl = [b_ref[i:i + 1, cs] for i in last]
            br = [b_ref[i:i + 1, cs] for i in mid]
            zero = jnp.zeros_like(bl[0])
            gpre = [zero, bl[0], bl[0] + bl[1], bl[0] + bl[1] + bl[2]]
            gtot = gpre[3] + bl[3]
            cr = [bl[c] - br[c] for c in range(nch)]
            e_d = [None,
                   [cr[0], cr[1], cr[2]],
                   [bl[1] + cr[0], bl[2] + cr[1]],
                   [bl[2] + bl[1] + cr[0]]]

            def bc(rows_list):
                return jnp.concatenate([jnp.broadcast_to(r, (HG_CHUNK, PACK)) for r in rows_list], axis=0)

            br_bc, g_bc = bc(br), bc(gpre)
            qe = q * jnp.exp2(b - br_bc)
            ke = (k * jnp.exp2(br_bc - b)).astype(BF16)
            qb = q * jnp.exp2(b)
            q_d = [qe] + [qb[d * HG_CHUNK:, :] * jnp.exp2(bc(e_d[d])) for d in range(1, nch)]
            qg = (qb * jnp.exp2(g_bc)).astype(BF16)
            k64 = (k * jnp.exp2(gtot - g_bc - b)).astype(BF16)
            ke_bd = jnp.concatenate([ke] * 4, axis=0) * bd4
            v_bd = jnp.concatenate([v] * 4, axis=0) * bd4
            qs = jnp.concatenate(q_d, axis=0).astype(BF16)
            s_all = lax.dot_general(qs, ke_bd, NT_DIMS, preferred_element_type=F32)
            yield
            start_d = [0, WIN, 2 * WIN - HG_CHUNK, 3 * WIN - 3 * HG_CHUNK]
            sc_chunks = []
            for c in range(nch):
                crow = slice(c * HG_CHUNK, (c + 1) * HG_CHUNK)
                sc_c = jnp.zeros((HG_CHUNK, PACK), F32)
                for d in range(c, -1, -1):
                    lo = start_d[d] + (c - d) * HG_CHUNK
                    sc_c = jnp.where(code[crow, :] == d, s_all[lo:lo + HG_CHUNK, :], sc_c)
                sc_chunks.append(sc_c)
            sc = jnp.concatenate(sc_chunks, axis=0)
            o = _dot(sc.astype(BF16), v_bd)
            st = st_ref[hg]
            if restart is not None:
                st = jnp.where(restart, 0.0, st)
            o = o + lax.dot_general(qg, st.astype(BF16) * bd4, NT_DIMS, preferred_element_type=F32)
            contrib = lax.dot_general(v, k64, TN_DIMS, preferred_element_type=F32)
            st_ref[hg] = st * jnp.exp2(gtot) + contrib
            oo = o * o
            head_of_lane = lax.broadcasted_iota(jnp.int32, (WIN, PACK), 1) // HEAD
            inv = jnp.zeros((WIN, PACK), F32)
            for h in range(PACK // HEAD):
                in_head = head_of_lane == h
                ms = jnp.sum(jnp.where(in_head, oo, 0.0), axis=-1, keepdims=True) * (1.0 / HEAD)
                inv = jnp.where(in_head, lax.rsqrt(ms + NORM_EPS), inv)
            gate = _silu(pin[rows, C_G + hg * PACK:C_G + (hg + 1) * PACK])
            yh = o * inv * hnw_ref[:, cs] * gate
            y_ref[rows, SSD_W + hg * PACK:SSD_W + (hg + 1) * PACK] = yh.astype(BF16)
            yield


def _mixer_body(xb_ref, xa0_ref, xa1_ref, nw_ref, win_ref, cw_ref, cb_ref, dtb_ref, alog_ref, dsk_ref,
                snw_ref, lbl_ref, hnw_ref, wout_ref, l64_ref, l16_ref, slm_ref, causal_ref, code_ref,
                bd4_ref,
                o_ref,
                p0_ref, p1_ref, xc0_ref, xc1_ref, hn_ref, tail_ref, act_ref, dt_ref, acum_ref, seg_ref,
                q_ref, k_ref, b_ref, y0_ref, y1_ref, sst_ref, st_ref, *, layer, blocks_per_seq):
    t = MIX_ROWS
    u = pl.program_id(0)

    refs = (cw_ref, cb_ref, dtb_ref, dsk_ref, snw_ref, hnw_ref, wout_ref, l64_ref, l16_ref, slm_ref, causal_ref,
            code_ref, bd4_ref, tail_ref, act_ref, dt_ref, acum_ref, seg_ref, q_ref, k_ref, b_ref,
            sst_ref, st_ref)

    def layer_consts():
        p = jax.nn.softmax(lbl_ref[...], axis=0)
        lb = jnp.zeros((1, HG_W), F32)
        for j in range(1, layer + 1):
            lb = lb + p[j:j + 1, :]
        return lb, -jnp.exp(alog_ref[...]) * LOG2_E

    @pl.when(u == 0)
    def _():
        tail_ref[...] = jnp.zeros_like(tail_ref)
        sst_ref[...] = jnp.zeros_like(sst_ref)
        st_ref[...] = jnp.zeros_like(st_ref)
        hn_ref[...] = _rms(xb_ref[0:t, :], nw_ref[...]).astype(BF16)
        for i in range(N_PROJ_TILES):
            _project_tile(i, hn_ref, win_ref, p0_ref, xc0_ref)
        for _ in _prep_stages(True, *layer_consts(), p0_ref, xc0_ref, refs):
            pass
        hn_ref[...] = _rms(xb_ref[t:2 * t, :], nw_ref[...]).astype(BF16)
        for i in range(N_EARLY_TILES):
            _project_tile(i, hn_ref, win_ref, p1_ref, xc1_ref)

    lb, a_neg = layer_consts()
    for half in range(2):
        pin, xcin = (p0_ref, xc0_ref) if half == 0 else (p1_ref, xc1_ref)
        pout, xcout = (p1_ref, xc1_ref) if half == 0 else (p0_ref, xc0_ref)
        xa_ref = xa0_ref if half == 0 else xa1_ref
        y_ref = y0_ref if half == 0 else y1_ref
        first = lax.rem(2 * u + half, blocks_per_seq) == 0
        first_next = lax.rem(2 * u + half + 1, blocks_per_seq) == 0
        brow = slice(half * t, (half + 1) * t)

        tile = N_EARLY_TILES
        for n_stage, _ in enumerate(_window_stages(first, pin, y_ref, refs)):
            if n_stage % 2 == 0 and tile < N_PROJ_TILES:
                _project_tile(tile, hn_ref, win_ref, pout, xcout)
                tile += 1
        assert tile == N_PROJ_TILES
        o_ref[brow, :] = xb_ref[brow, :] + _dot(y_ref[...], wout_ref[...])
        hn_ref[...] = _rms(xa_ref[...], nw_ref[...]).astype(BF16)
        per_stage = N_EARLY_TILES // 3
        for i, _ in enumerate(_prep_stages(first_next, lb, a_neg, pout, xcout, refs)):
            for j in range(i * per_stage, (i + 1) * per_stage):
                _project_tile(j, hn_ref, win_ref, pin, xcin)


def _mixer_consts():
    t = MIX_ROWS
    r = np.arange(t)
    same64 = (r[:, None] // WIN) == (r[None, :] // WIN)
    same16 = (r[:, None] // HG_CHUNK) == (r[None, :] // HG_CHUNK)
    lower = r[None, :] <= r[:, None]
    l64 = (same64 & lower).astype(np.float32)
    l16 = (same16 & lower).astype(np.float32)
    l64 = np.concatenate([l64, l64], axis=1)
    l16 = np.concatenate([l16, l16], axis=1)
    lane = np.arange(SSD_W)
    slm = ((r[:, None] % WIN) > (lane[None, :] % HEAD)).astype(np.float32)
    tt = np.arange(WIN)
    causal = (tt[:, None] >= (lane[None, :] % HEAD)).astype(np.float32)
    ls = np.arange(PACK) % HEAD
    dch = tt[:, None] // HG_CHUNK - ls[None, :] // HG_CHUNK
    code = np.where(dch == 0, np.where(ls[None, :] <= tt[:, None], 0, 9), np.where((dch >= 1) & (dch <= 3), dch, 9))
    rp = np.arange(PACK)
    bd4 = ((rp[:, None] // HEAD) == (rp[None, :] // HEAD)).astype(np.float32)
    return (jnp.asarray(l64, BF16), jnp.asarray(l16, BF16), jnp.asarray(slm, F32), jnp.asarray(causal, F32),
            jnp.asarray(code.astype(np.float32), F32), jnp.asarray(bd4, BF16))


def _mixer(x2d, nw, win, cw, cb, dtb, alog, dsk, snw, lbl, hnw, wout, consts, layer, seq):
    n = x2d.shape[0]
    t = MIX_ROWS
    nblk = n // t
    const2 = lambda u: (0, 0)
    lsel = lambda u: (layer, 0, 0)
    full = lambda a: pl.BlockSpec(a.shape, const2)
    l64, l16, slm, causal, code, bd4 = consts
    return pl.pallas_call(
        functools.partial(_mixer_body, layer=layer, blocks_per_seq=seq // t),
        grid=(nblk // 2,),
        in_specs=[
            pl.BlockSpec((2 * t, D_MODEL), lambda u: (u, 0)),
            pl.BlockSpec((t, D_MODEL), lambda u: (jnp.minimum(2 * u + 2, nblk - 1), 0)),
            pl.BlockSpec((t, D_MODEL), lambda u: (jnp.minimum(2 * u + 3, nblk - 1), 0)),
            pl.BlockSpec((None, 1, D_MODEL), lsel),
            pl.BlockSpec((None, D_MODEL, D_PROJ), lsel),
            pl.BlockSpec((None, CONV_K, CONV_DIM), lsel),
            pl.BlockSpec((None, 1, CONV_DIM), lsel),
            pl.BlockSpec((None, 1, SSD_W), lsel),
            pl.BlockSpec((None, 1, SSD_W), lsel),
            pl.BlockSpec((None, 1, SSD_W), lsel),
            pl.BlockSpec((None, 1, SSD_W), lsel),
            full(lbl),
            pl.BlockSpec((None, 1, HG_W), lsel),
            pl.BlockSpec((None, D_MODEL, D_MODEL), lsel),
            full(l64), full(l16), full(slm), full(causal), full(code), full(bd4),
        ],
        out_specs=pl.BlockSpec((2 * t, D_MODEL), lambda u: (u, 0)),
        out_shape=jax.ShapeDtypeStruct(x2d.shape, F32),
        scratch_shapes=[
            pltpu.VMEM((t, PROJ_W), F32),
            pltpu.VMEM((t, PROJ_W), F32),
            pltpu.VMEM((t + CONV_PAD, CONV_DIM), F32),
            pltpu.VMEM((t + CONV_PAD, CONV_DIM), F32),
            pltpu.VMEM((t, D_MODEL), BF16),
            pltpu.VMEM((CONV_PAD, CONV_DIM), F32),
            pltpu.VMEM((t, CONV_DIM), F32),
            pltpu.VMEM((t, SSD_W), F32),
            pltpu.VMEM((t, SSD_W), F32),
            pltpu.VMEM((t, SSD_W), F32),
            pltpu.VMEM((t, HG_W), F32),
            pltpu.VMEM((t, HG_W), F32),
            pltpu.VMEM((t, HG_W), F32),
            pltpu.VMEM((t, D_MODEL), BF16),
            pltpu.VMEM((t, D_MODEL), BF16),
            pltpu.VMEM((HEAD, SSD_W), F32),
            pltpu.VMEM((2, PACK, PACK), F32),
        ],
        compiler_params=pltpu.CompilerParams(
            dimension_semantics=("arbitrary",), vmem_limit_bytes=VMEM_LIMIT),
        name="token_mixer",
    )(x2d, x2d, x2d, nw, win, cw, cb, dtb, alog, dsk, snw, lbl, hnw, wout, l64, l16, slm, causal, code, bd4)


def kernel(x, ffn1_norm, ffn1_w_gate, ffn1_w_up, ffn1_w_down, mix_norm, w_in, ssd_conv_w, ssd_conv_b,
           ssd_dt_bias, ssd_a_log, ssd_d, ssd_norm_w, hg_lb_logits, hg_norm_w, w_out, ffn2_norm,
           ffn2_w_gate, ffn2_w_up, ffn2_w_down, final_norm):
    bsz, seq, d_model = x.shape
    depth = w_in.shape[0]
    assert d_model == D_MODEL and w_in.shape[1:] == (D_MODEL, D_PROJ_IN) and ffn1_w_gate.shape[1:] == (D_MODEL, D_FF)
    assert seq % MIX_ROWS == 0 and (bsz * seq) % (2 * MIX_ROWS) == 0 and (bsz * seq) % FFN_ROWS == 0
    rep = lambda a: jnp.repeat(a.astype(F32), HEAD, axis=-1)[:, None, :]
    row = lambda a: a.astype(F32)[:, None, :]
    w_dt = jnp.repeat(w_in[:, :, W_IN_DT0:W_IN_DT1], HEAD, axis=-1)
    win_k = jnp.concatenate([w_in[:, :, 0:W_IN_DT0], w_in[:, :, W_IN_DT1:D_PROJ_IN], w_dt], axis=-1).astype(BF16)
    wout_k = w_out.astype(BF16)
    f1 = (row(ffn1_norm), ffn1_w_gate.astype(BF16), ffn1_w_up.astype(BF16), ffn1_w_down.astype(BF16))
    f2 = (row(ffn2_norm), ffn2_w_gate.astype(BF16), ffn2_w_up.astype(BF16), ffn2_w_down.astype(BF16))
    fnw = final_norm.astype(F32)[None, :]
    hnw = jnp.tile(hg_norm_w.astype(F32), (1, HG_W // HEAD))[:, None, :]
    consts = _mixer_consts()

    xf = x.astype(F32).reshape(bsz * seq, D_MODEL)
    for l in range(depth):
        xf = _ffn(xf, *f1, fnw, l, False)
        xf = _mixer(xf, row(mix_norm), win_k, ssd_conv_w.astype(F32), row(ssd_conv_b), rep(ssd_dt_bias),
                    rep(ssd_a_log), rep(ssd_d), row(ssd_norm_w), hg_lb_logits.astype(F32), hnw, wout_k,
                    consts, l, seq)
        xf = _ffn(xf, *f2, fnw, l, l == depth - 1)
    return xf.reshape(bsz, seq, D_MODEL).astype(x.dtype)
```
